```python
import functools
import jax, jax.numpy as jnp
from jax import lax
import numpy as np

D_MODEL = 2048
BATCH = 4
SEQ = 2048
DEPTH = 2
DEC_BATCH = 32
DEC_SEQ = 4
PAST_LEN = 8192
PAGE_SIZE = 128

N_META = 16
SB_HEADS = 8
SB_HEAD_DIM = 128
SB_WIDTH = SB_HEADS * SB_HEAD_DIM
SB_BLOCK = 128
SB_BIAS_INIT = -8.0
GLA_HEADS = 4
GLA_DK = 128
GLA_DV = 256
GLA_KW = GLA_HEADS * GLA_DK
GLA_VW = GLA_HEADS * GLA_DV
GLA_RANK = 16
GLA_TAU = 16.0
GLA_CHUNK = 128
N_GROUPS = 4
EXPERTS_PER_GROUP = 8
N_EXPERTS = N_GROUPS * EXPERTS_PER_GROUP
TOP_K = 2
D_EXPERT = 256
EPS = 1e-6
IN_SIZES = (SB_WIDTH, SB_WIDTH, SB_WIDTH, GLA_KW, GLA_KW, GLA_VW, GLA_VW, GLA_RANK, D_MODEL, D_MODEL)
IN_SPLITS = tuple(int(s) for s in np.cumsum(IN_SIZES)[:-1])
N_IN = int(sum(IN_SIZES))

kernel_name = 'hybrid_stickbreak_gla_hiermoe_step'


def rms_norm(x, g):
    xf = x.astype(jnp.float32)
    y = xf * lax.rsqrt(jnp.mean(xf * xf, axis=-1, keepdims=True) + EPS)
    return (y * g.astype(jnp.float32)).astype(x.dtype)


def stick_breaking(q, k, v, q_pos, k_pos, bias):
    z = (jnp.einsum('bqhd,bkhd->bhqk', q, k).astype(jnp.float32) * (SB_HEAD_DIM ** -0.5)
         + bias.astype(jnp.float32)[None, :, None, None])
    mask = (k_pos[None, :] < q_pos[:, None])[None, None]
    log_stay = jnp.where(mask, jax.nn.log_sigmoid(-z), 0.0)
    later = lax.cumsum(log_stay, axis=3, reverse=True) - log_stay
    w = jnp.where(mask, jnp.exp(jax.nn.log_sigmoid(z) + later), 0.0)
    return jnp.einsum('bhqk,bkhd->bqhd', w.astype(v.dtype), v)


def sb_prompt(q, k, v, bias):
    bsz, length = q.shape[0], q.shape[1]
    pos = jnp.arange(length, dtype=jnp.int32)
    o_meta = stick_breaking(q[:, :N_META], k[:, :N_META], v[:, :N_META], pos[:N_META], pos[:N_META], bias)
    n_blk = (length - N_META) // SB_BLOCK
    q_blk = q[:, N_META:].reshape(bsz, n_blk, SB_BLOCK, SB_HEADS, SB_HEAD_DIM).transpose(1, 0, 2, 3, 4)
    p_blk = pos[N_META:].reshape(n_blk, SB_BLOCK)
    o_blk = lax.map(lambda a: stick_breaking(a[0], k, v, a[1], pos, bias), (q_blk, p_blk))
    o_real = o_blk.transpose(1, 0, 2, 3, 4).reshape(bsz, length - N_META, SB_HEADS, SB_HEAD_DIM)
    return jnp.concatenate([o_meta, o_real], axis=1)


def gla_chunk(q, k, v, g, s0):
    t = q.shape[2]
    qf, kf, vf = q.astype(jnp.float32), k.astype(jnp.float32), v.astype(jnp.float32)
    b = jnp.cumsum(g.astype(jnp.float32), axis=2)
    inter = jnp.einsum('bhtk,bhkv->bhtv', qf * jnp.exp(b), s0)
    causal = jnp.arange(t)[:, None] >= jnp.arange(t)[None, :]
    diff = b[:, :, :, None, :] - b[:, :, None, :, :]
    decay = jnp.where(causal[None, None, :, :, None], jnp.exp(jnp.minimum(diff, 0.0)), 0.0)
    scores = jnp.einsum('bhtk,bhsk,bhtsk->bhts', qf, kf, decay)
    o = inter + jnp.einsum('bhts,bhsv->bhtv', scores, vf)
    b_last = b[:, :, -1:, :]
    s_new = jnp.exp(b_last[:, :, 0, :, None]) * s0 + jnp.einsum('bhsk,bhsv->bhkv', kf * jnp.exp(b_last - b), vf)
    return o, s_new


def gla_prompt(q, k, v, g):
    bsz, heads, length = q.shape[0], q.shape[1], q.shape[2]
    s0 = jnp.zeros((bsz, heads, GLA_DK, GLA_DV), jnp.float32)
    o_meta, s = gla_chunk(q[:, :, :N_META], k[:, :, :N_META], v[:, :, :N_META], g[:, :, :N_META], s0)
    n_c = (length - N_META) // GLA_CHUNK

    def to_chunks(a):
        return a[:, :, N_META:].reshape(bsz, heads, n_c, GLA_CHUNK, a.shape[-1]).transpose(2, 0, 1, 3, 4)

    def step(state, c):
        o, state = gla_chunk(c[0], c[1], c[2], c[3], state)
        return state, o

    s_fin, o_c = lax.scan(step, s, (to_chunks(q), to_chunks(k), to_chunks(v), to_chunks(g)))
    o_real = o_c.transpose(1, 2, 0, 3, 4).reshape(bsz, heads, length - N_META, GLA_DV)
    return jnp.concatenate([o_meta, o_real], axis=2), s_fin


def project(n, w_in, w_a2, b_a):
    bsz, t = n.shape[0], n.shape[1]
    z = n @ w_in
    qa, ka, va, qb, kb, vb, rb, ab, gate_a, gate_b = jnp.split(z, IN_SPLITS, axis=-1)
    sb = lambda a: a.reshape(bsz, t, SB_HEADS, SB_HEAD_DIM)
    hd = lambda a, d: a.reshape(bsz, t, GLA_HEADS, d).transpose(0, 2, 1, 3)
    log_a = jax.nn.log_sigmoid((ab @ w_a2 + b_a).astype(jnp.float32)) / GLA_TAU
    return (sb(qa), sb(ka), sb(va), hd(qb, GLA_DK) * (GLA_DK ** -0.5), hd(kb, GLA_DK), hd(vb, GLA_DV),
            rb, hd(log_a, GLA_DK), gate_a, gate_b)


def merge_branches(o_a, o_b, rb, gate_a, gate_b, gla_g, w_a_up, w_b_up, w_out):
    bsz, t = rb.shape[0], rb.shape[1]
    a = o_a.reshape(bsz, t, SB_WIDTH) @ w_a_up
    o_b = rms_norm(o_b.astype(rb.dtype), gla_g)
    o_b = o_b.transpose(0, 2, 1, 3).reshape(bsz, t, GLA_VW) * jax.nn.silu(rb)
    b = o_b @ w_b_up
    return (jax.nn.sigmoid(gate_a) * a + jax.nn.sigmoid(gate_b) * b) @ w_out


def hier_moe(x, w_rg, b_rg, w_re, b_re, w_g, w_u, w_d):
    shp = x.shape
    t = x.reshape(-1, D_MODEL)
    g_logits = (t @ w_rg).astype(jnp.float32) + b_rg.astype(jnp.float32)
    g_prob = jax.nn.softmax(g_logits, axis=-1)
    g_idx = jnp.argmax(g_logits, axis=-1)
    g_w = jnp.take_along_axis(g_prob, g_idx[:, None], axis=1)
    e_logits = ((t @ w_re).astype(jnp.float32) + b_re.astype(jnp.float32)).reshape(-1, N_GROUPS, EXPERTS_PER_GROUP)
    e_in = jnp.take_along_axis(e_logits, g_idx[:, None, None], axis=1)[:, 0]
    top_v, top_i = lax.top_k(e_in, TOP_K)
    w = jax.nn.softmax(top_v, axis=-1) * g_w
    eid = g_idx[:, None] * EXPERTS_PER_GROUP + top_i
    combine = jnp.einsum('nk,nke->ne', w, jax.nn.one_hot(eid, N_EXPERTS, dtype=jnp.float32))
    h = jax.nn.silu(jnp.einsum('nd,edf->nef', t, w_g)) * jnp.einsum('nd,edf->nef', t, w_u)
    y = jnp.einsum('nef,efd->nd', h * combine[:, :, None].astype(h.dtype), w_d)
    return y.reshape(shp)


def prompt_mixers(qa, ka, va, qb, kb, vb, lg, sb_b):
    o_b, s_fin = gla_prompt(qb, kb, vb, lg)
    return sb_prompt(qa, ka, va, sb_b), o_b, s_fin


def sample_mixers(qa, ka, va, qb, kb, vb, lg, sb_b, k_pages, v_pages, s_prev, page_table):
    dec_b, dec_t = qa.shape[0], qa.shape[1]
    past_len = page_table.shape[1] * k_pages.shape[1]
    k_past = k_pages[page_table].reshape(dec_b, past_len, SB_HEADS, SB_HEAD_DIM).astype(ka.dtype)
    v_past = v_pages[page_table].reshape(dec_b, past_len, SB_HEADS, SB_HEAD_DIM).astype(va.dtype)
    k_all = jnp.concatenate([k_past, ka], axis=1)
    v_all = jnp.concatenate([v_past, va], axis=1)
    q_pos = past_len + jnp.arange(dec_t, dtype=jnp.int32)
    k_pos = jnp.arange(past_len + dec_t, dtype=jnp.int32)
    o_a = stick_breaking(qa, k_all, v_all, q_pos, k_pos, sb_b)
    o_b, s_new = gla_chunk(qb, kb, vb, lg, s_prev.astype(jnp.float32))
    return o_a, o_b, s_new


def layer(h, mixers, p):
    (n1, w_in, sb_b, w_a2, b_a, gla_g, w_a_up, w_b_up, w_out, n2, w_rg, b_rg, w_re, b_re, w_g, w_u, w_d) = p
    qa, ka, va, qb, kb, vb, rb, lg, gate_a, gate_b = project(rms_norm(h, n1), w_in, w_a2, b_a)
    o_a, o_b, s_new = mixers(qa, ka, va, qb, kb, vb, lg, sb_b)
    h = h + merge_branches(o_a, o_b, rb, gate_a, gate_b, gla_g, w_a_up, w_b_up, w_out)
    h = h + hier_moe(rms_norm(h, n2), w_rg, b_rg, w_re, b_re, w_g, w_u, w_d)
    return h, ka, va, s_new


def setup_inputs(seed: int = 0) -> dict:
    key = jax.random.key(seed)
    ks = iter(jax.random.split(key, 32))
    n_pages = PAST_LEN // PAGE_SIZE
    n_used = DEC_BATCH * n_pages
    n_phys = n_used + max(1, n_used // 4)
    nrm = lambda shape, scale: jax.random.normal(next(ks), shape, jnp.float32) * scale
    gain = lambda shape: 1.0 + 0.05 * jax.random.normal(next(ks), shape, jnp.float32)
    page_table = jax.random.permutation(next(ks), n_phys)[:n_used].reshape(DEC_BATCH, n_pages).astype(jnp.int32)
    return {
        'x_prompt': nrm((BATCH, SEQ, D_MODEL), 1.0),
        'x_sample': nrm((DEC_BATCH, DEC_SEQ, D_MODEL), 1.0),
        'cache_k': nrm((DEPTH, n_phys, PAGE_SIZE, SB_HEADS, SB_HEAD_DIM), 1.0),
        'cache_v': nrm((DEPTH, n_phys, PAGE_SIZE, SB_HEADS, SB_HEAD_DIM), 1.0),
        'state_gla': nrm((DEPTH, DEC_BATCH, GLA_HEADS, GLA_DK, GLA_DV), 1.0),
        'page_table': page_table,
        'meta': nrm((N_META, D_MODEL), 1.0),
        'norm1_g': gain((DEPTH, D_MODEL)),
        'w_in': nrm((DEPTH, D_MODEL, N_IN), D_MODEL ** -0.5),
        'sb_bias': SB_BIAS_INIT + nrm((DEPTH, SB_HEADS), 0.1),
        'gla_w_a2': nrm((DEPTH, GLA_RANK, GLA_KW), GLA_RANK ** -0.5),
        'gla_b_a': nrm((DEPTH, GLA_KW), 0.1),
        'gla_norm_g': gain((DEPTH, GLA_DV)),
        'w_a_up': nrm((DEPTH, SB_WIDTH, D_MODEL), SB_WIDTH ** -0.5),
        'w_b_up': nrm((DEPTH, GLA_VW, D_MODEL), GLA_VW ** -0.5),
        'w_out': nrm((DEPTH, D_MODEL, D_MODEL), D_MODEL ** -0.5),
        'norm2_g': gain((DEPTH, D_MODEL)),
        'w_router_group': nrm((DEPTH, D_MODEL, N_GROUPS), D_MODEL ** -0.5),
        'b_router_group': nrm((DEPTH, N_GROUPS), 0.01),
        'w_router_expert': nrm((DEPTH, D_MODEL, N_EXPERTS), D_MODEL ** -0.5),
        'b_router_expert': nrm((DEPTH, N_EXPERTS), 0.01),
        'w_exp_gate': nrm((DEPTH, N_EXPERTS, D_MODEL, D_EXPERT), D_MODEL ** -0.5),
        'w_exp_up': nrm((DEPTH, N_EXPERTS, D_MODEL, D_EXPERT), D_MODEL ** -0.5),
        'w_exp_down': nrm((DEPTH, N_EXPERTS, D_EXPERT, D_MODEL), D_EXPERT ** -0.5),
        'final_norm_g': gain((D_MODEL,)),
    }


def reference(x_prompt, x_sample, cache_k, cache_v, state_gla, page_table, meta, norm1_g, w_in, sb_bias,
              gla_w_a2, gla_b_a, gla_norm_g, w_a_up, w_b_up, w_out, norm2_g, w_router_group, b_router_group,
              w_router_expert, b_router_expert, w_exp_gate, w_exp_up, w_exp_down, final_norm_g):
    bsz = x_prompt.shape[0]
    meta_rows = jnp.broadcast_to(meta.astype(x_prompt.dtype)[None], (bsz, N_META, D_MODEL))
    h_p = jnp.concatenate([meta_rows, x_prompt], axis=1)
    h_s = x_sample
    kp_rows, vp_rows, ks_rows, vs_rows, sp_list, ss_list = [], [], [], [], [], []
    for l in range(DEPTH):
        p = (norm1_g[l], w_in[l], sb_bias[l], gla_w_a2[l], gla_b_a[l], gla_norm_g[l], w_a_up[l], w_b_up[l],
             w_out[l], norm2_g[l], w_router_group[l], b_router_group[l], w_router_expert[l], b_router_expert[l],
             w_exp_gate[l], w_exp_up[l], w_exp_down[l])
        h_p, kp, vp, sp = layer(h_p, prompt_mixers, p)
        mix_s = functools.partial(sample_mixers, k_pages=cache_k[l], v_pages=cache_v[l],
                                  s_prev=state_gla[l], page_table=page_table)
        h_s, ks_, vs_, ss = layer(h_s, mix_s, p)
        kp_rows.append(kp)
        vp_rows.append(vp)
        ks_rows.append(ks_)
        vs_rows.append(vs_)
        sp_list.append(sp)
        ss_list.append(ss)
    y_prompt = rms_norm(h_p, final_norm_g)[:, N_META:]
    y_sample = rms_norm(h_s, final_norm_g)
    k_new_prompt = jnp.stack(kp_rows)
    v_new_prompt = jnp.stack(vp_rows)
    k_new_sample = jnp.stack(ks_rows)
    v_new_sample = jnp.stack(vs_rows)
    gla_state_prompt = jnp.stack(sp_list)
    gla_state_sample = jnp.stack(ss_list)
    return (y_prompt, y_sample, k_new_prompt, v_new_prompt, k_new_sample, v_new_sample, gla_state_prompt, gla_state_sample)
```

```python
import functools

import numpy as np
import jax
import jax.numpy as jnp
from jax import lax
from jax.experimental import pallas as pl
from jax.experimental.pallas import tpu as pltpu

F32 = jnp.float32
BF16 = jnp.bfloat16

N_META = 16
SB_HEADS = 8
SB_HEAD_DIM = 128
SB_WIDTH = SB_HEADS * SB_HEAD_DIM
GLA_HEADS = 4
GLA_DK = 128
GLA_DV = 256
GLA_KW = GLA_HEADS * GLA_DK
GLA_VW = GLA_HEADS * GLA_DV
GLA_RANK = 16
GLA_TAU = 16.0
N_GROUPS = 4
EXPERTS_PER_GROUP = 8
N_EXPERTS = N_GROUPS * EXPERTS_PER_GROUP
D_EXPERT = 256
EPS = 1e-6

BLK = 128
FRONT_PAD = BLK - N_META
SAMPLE_ROWS = 8
COL_QA, COL_KA, COL_VA = 0, SB_WIDTH, 2 * SB_WIDTH
COL_QB = 3 * SB_WIDTH
COL_KB = COL_QB + GLA_KW
COL_VB = COL_KB + GLA_KW
COL_RB = COL_VB + GLA_VW
N_MAIN = COL_RB + GLA_VW
COL_AB = N_MAIN
COL_GATES = N_MAIN + GLA_RANK

EXPERT_TILE = 256
VMEM_LIMIT = 48 * 1024 * 1024
NEG_BIG = -1e30


def _params(*sem):
    return pltpu.CompilerParams(dimension_semantics=sem, vmem_limit_bytes=VMEM_LIMIT)


def _pick(m, prefs):
    for p in prefs:
        if m % p == 0:
            return p
    return m


def _softplus(z):
    return jnp.maximum(z, 0.0) + jnp.log1p(jnp.exp(-jnp.abs(z)))


def _sigmoid(z):
    return 1.0 / (1.0 + jnp.exp(-z))


def _split_bf16(x):
    hi = x.astype(BF16)
    lo = (x - hi.astype(F32)).astype(BF16)
    return hi, lo


def _rms_kernel(x_ref, g_ref, o_ref):
    x = x_ref[...]
    ms = jnp.mean(x * x, axis=-1, keepdims=True)
    o_ref[...] = (x * lax.rsqrt(ms + EPS) * g_ref[...]).astype(o_ref.dtype)


def rms_norm_rows(x, g, out_dtype):
    m, d = x.shape
    tm = _pick(m, (256, 128, 64, 16))
    return pl.pallas_call(
        _rms_kernel,
        out_shape=jax.ShapeDtypeStruct((m, d), out_dtype),
        grid=(m // tm,),
        in_specs=[pl.BlockSpec((tm, d), lambda i: (i, 0)), pl.BlockSpec((1, d), lambda i: (0, 0))],
        out_specs=pl.BlockSpec((tm, d), lambda i: (i, 0)),
        compiler_params=_params("parallel"),
    )(x, g.reshape(1, d))


def _mm_kernel(x_ref, w_ref, o_ref):
    o_ref[...] = jnp.dot(x_ref[...], w_ref[...].astype(BF16), preferred_element_type=F32).astype(o_ref.dtype)


def _mm_res_kernel(x_ref, w_ref, r_ref, o_ref):
    o_ref[...] = r_ref[...] + jnp.dot(x_ref[...], w_ref[...].astype(BF16), preferred_element_type=F32)


def matmul(x, w, n_out, out_dtype=F32, residual=None, layer=None):
    m, k = x.shape
    tm = _pick(m, (1792, 1024, 896, 512, 256, 128))
    tn = _pick(n_out, (512, 256, 128))
    if layer is None:
        w_spec = pl.BlockSpec((k, tn), lambda i, j: (0, j))
    else:
        w_spec = pl.BlockSpec((None, k, tn), lambda i, j: (layer, 0, j))
    in_specs = [pl.BlockSpec((tm, k), lambda i, j: (i, 0)), w_spec]
    args = [x, w]
    body = _mm_kernel
    if residual is not None:
        in_specs.append(pl.BlockSpec((tm, tn), lambda i, j: (i, j)))
        args.append(residual)
        body = _mm_res_kernel
    return pl.pallas_call(
        body,
        out_shape=jax.ShapeDtypeStruct((m, n_out), out_dtype),
        grid=(m // tm, n_out // tn),
        in_specs=in_specs,
        out_specs=pl.BlockSpec((tm, tn), lambda i, j: (i, j)),
        compiler_params=_params("parallel", "parallel"),
    )(*args)


def _lg_kernel(x_ref, wab_ref, wa2_ref, ba_ref, o_ref):
    ab = jnp.dot(x_ref[...], wab_ref[...].astype(BF16), preferred_element_type=F32)
    z = jnp.dot(ab.astype(BF16), wa2_ref[...].astype(BF16), preferred_element_type=F32) + ba_ref[...]
    o_ref[...] = -_softplus(-z) * (1.0 / GLA_TAU)


def forget_gate(xn, w_ab, w_a2, b_a):
    m, d = xn.shape
    tm = _pick(m, (896, 512, 256, 128))
    return pl.pallas_call(
        _lg_kernel,
        out_shape=jax.ShapeDtypeStruct((m, GLA_KW), F32),
        grid=(m // tm,),
        in_specs=[pl.BlockSpec((tm, d), lambda i: (i, 0)),
                  pl.BlockSpec((d, BLK), lambda i: (0, 0)),
                  pl.BlockSpec((BLK, GLA_KW), lambda i: (0, 0)),
                  pl.BlockSpec((1, GLA_KW), lambda i: (0, 0))],
        out_specs=pl.BlockSpec((tm, GLA_KW), lambda i: (i, 0)),
        compiler_params=_params("parallel"),
    )(xn, w_ab, w_a2, b_a.reshape(1, GLA_KW))


def _sb_prompt_kernel(bias_ref, q_ref, k_ref, v_ref, o_ref):
    h = pl.program_id(1)
    qi = pl.program_id(2)
    q = q_ref[...].astype(BF16)
    bias = bias_ref[h]
    row = lax.broadcasted_iota(jnp.int32, (BLK, BLK), 0)
    col = lax.broadcasted_iota(jnp.int32, (BLK, BLK), 1)
    later_mat = jnp.where(row > col, 1.0, 0.0).astype(BF16)
    q_pos = qi * BLK + row
    scale = SB_HEAD_DIM ** -0.5

    def body(i, carry):
        acc, run = carry
        kb = qi - i
        off = pl.multiple_of(kb * BLK, BLK)
        k_blk = k_ref[pl.ds(off, BLK), :].astype(BF16)
        v_blk = v_ref[pl.ds(off, BLK), :].astype(BF16)
        z = lax.dot_general(q, k_blk, (((1,), (1,)), ((), ())), preferred_element_type=F32) * scale + bias
        k_pos = kb * BLK + col
        mask = jnp.logical_and(k_pos < q_pos, k_pos >= FRONT_PAD)
        sp = _softplus(z)
        log_stay = jnp.where(mask, -sp, 0.0)
        hi, lo = _split_bf16(log_stay)
        later = (run + jnp.dot(hi, later_mat, preferred_element_type=F32)
                 + jnp.dot(lo, later_mat, preferred_element_type=F32))
        w = jnp.where(mask, jnp.exp(z - sp + later), 0.0)
        acc = acc + jnp.dot(w.astype(BF16), v_blk, preferred_element_type=F32)
        run = run + jnp.sum(log_stay, axis=1, keepdims=True)
        return acc, run

    acc, _ = lax.fori_loop(0, qi + 1, body,
                           (jnp.zeros((BLK, SB_HEAD_DIM), F32), jnp.zeros((BLK, 1), F32)))
    o_ref[...] = acc.astype(o_ref.dtype)


def sb_prompt(z_main, sb_bias, n_batch, rows_per_seq, m_total):
    nq = rows_per_seq // BLK
    kcol = COL_KA // SB_HEAD_DIM
    vcol = COL_VA // SB_HEAD_DIM
    return pl.pallas_call(
        _sb_prompt_kernel,
        out_shape=jax.ShapeDtypeStruct((m_total, SB_WIDTH), BF16),
        grid_spec=pltpu.PrefetchScalarGridSpec(
            num_scalar_prefetch=0,
            grid=(n_batch, SB_HEADS, nq),
            in_specs=[pl.BlockSpec(memory_space=pltpu.SMEM),
                      pl.BlockSpec((BLK, SB_HEAD_DIM), lambda b, h, i: (b * nq + i, h)),
                      pl.BlockSpec((rows_per_seq, SB_HEAD_DIM), lambda b, h, i: (b, kcol + h)),
                      pl.BlockSpec((rows_per_seq, SB_HEAD_DIM), lambda b, h, i: (b, vcol + h))],
            out_specs=pl.BlockSpec((BLK, SB_HEAD_DIM), lambda b, h, i: (b * nq + i, h)),
        ),
        compiler_params=_params("parallel", "parallel", "arbitrary"),
    )(sb_bias, z_main, z_main, z_main)


def _sb_paged_kernel(pages_per_step, pt_ref, wq_ref, bias_ref, knew_ref, vnew_ref, *rest):
    k_refs = rest[:pages_per_step]
    v_refs = rest[pages_per_step:2 * pages_per_step]
    o_ref, acc_ref, run_ref = rest[2 * pages_per_step:]
    p = pl.program_id(1)
    n_hq = wq_ref.shape[0]
    wq = wq_ref[...]
    bias = bias_ref[...]
    row = lax.broadcasted_iota(jnp.int32, (BLK, BLK), 0)
    col = lax.broadcasted_iota(jnp.int32, (BLK, BLK), 1)
    later_mat = jnp.where(row > col, 1.0, 0.0).astype(BF16)
    scale = SB_HEAD_DIM ** -0.5

    def process(k_blk, v_blk, mask):
        z = lax.dot_general(wq, k_blk.astype(BF16), (((1,), (1,)), ((), ())),
                            preferred_element_type=F32) * scale + bias
        sp = _softplus(z)
        log_stay = -sp if mask is None else jnp.where(mask, -sp, 0.0)
        hi, lo = _split_bf16(log_stay)
        later = (run_ref[...] + jnp.dot(hi, later_mat, preferred_element_type=F32)
                 + jnp.dot(lo, later_mat, preferred_element_type=F32))
        w = jnp.exp(z - sp + later)
        if mask is not None:
            w = jnp.where(mask, w, 0.0)
        acc_ref[...] += jnp.dot(w.astype(BF16), v_blk.astype(BF16), preferred_element_type=F32)
        run_ref[...] += jnp.sum(log_stay, axis=1, keepdims=True)

    @pl.when(p == 0)
    def _():
        acc_ref[...] = jnp.zeros_like(acc_ref)
        run_ref[...] = jnp.zeros_like(run_ref)
        c = lax.broadcasted_iota(jnp.int32, (n_hq, BLK), 0)
        key = lax.broadcasted_iota(jnp.int32, (n_hq, BLK), 1)
        process(knew_ref[...], vnew_ref[...], key < (c >> 3))

    for j in range(pages_per_step):
        process(k_refs[j][...], v_refs[j][...], None)

    @pl.when(p == pl.num_programs(1) - 1)
    def _():
        sub = lax.broadcasted_iota(jnp.int32, (SB_HEADS, SB_WIDTH), 0)
        lane = lax.broadcasted_iota(jnp.int32, (SB_HEADS, SB_WIDTH), 1)
        own = sub == (lane >> 7)
        n_t = n_hq // SB_HEADS
        for t in range(n_t):
            blk = acc_ref[t * SB_HEADS:(t + 1) * SB_HEADS, :]
            o_ref[t:t + 1, :] = jnp.sum(jnp.where(own, blk, 0.0), axis=0, keepdims=True).astype(o_ref.dtype)
        o_ref[n_t:, :] = jnp.zeros((o_ref.shape[0] - n_t, SB_WIDTH), o_ref.dtype)


def sb_paged(q_s, k_new, v_new, cache_k, cache_v, page_table, sb_bias, layer):
    n_seq, n_t = q_s.shape[0], q_s.shape[1]
    n_pages = page_table.shape[1]
    page = cache_k.shape[2]
    assert page == BLK
    pps = 2 if n_pages % 2 == 0 else 1
    n_hq = n_t * SB_HEADS
    eye = jnp.eye(SB_HEADS, dtype=F32)
    wq = jnp.einsum("sthd,hg->sthgd", q_s, eye).reshape(n_seq, n_hq, SB_WIDTH).astype(BF16)
    bias = jnp.broadcast_to(jnp.tile(sb_bias.astype(F32), n_t)[:, None], (n_hq, BLK))
    pad = ((0, 0), (0, BLK - n_t), (0, 0))
    k_new = jnp.pad(k_new, pad)
    v_new = jnp.pad(v_new, pad)
    ck = cache_k.reshape(cache_k.shape[0], cache_k.shape[1], page, SB_WIDTH)
    cv = cache_v.reshape(cache_v.shape[0], cache_v.shape[1], page, SB_WIDTH)

    def page_spec(j):
        return pl.BlockSpec((None, None, page, SB_WIDTH),
                            lambda s, p, pt: (layer, pt[s, n_pages - 1 - (p * pps + j)], 0, 0))

    in_specs = ([pl.BlockSpec((None, n_hq, SB_WIDTH), lambda s, p, pt: (s, 0, 0)),
                 pl.BlockSpec((n_hq, BLK), lambda s, p, pt: (0, 0)),
                 pl.BlockSpec((None, BLK, SB_WIDTH), lambda s, p, pt: (s, 0, 0)),
                 pl.BlockSpec((None, BLK, SB_WIDTH), lambda s, p, pt: (s, 0, 0))]
                + [page_spec(j) for j in range(pps)] + [page_spec(j) for j in range(pps)])
    out = pl.pallas_call(
        functools.partial(_sb_paged_kernel, pps),
        out_shape=jax.ShapeDtypeStruct((n_seq, SAMPLE_ROWS, SB_WIDTH), BF16),
        grid_spec=pltpu.PrefetchScalarGridSpec(
            num_scalar_prefetch=1,
            grid=(n_seq, n_pages // pps),
            in_specs=in_specs,
            out_specs=pl.BlockSpec((None, SAMPLE_ROWS, SB_WIDTH), lambda s, p, pt: (s, 0, 0)),
            scratch_shapes=[pltpu.VMEM((n_hq, SB_WIDTH), F32), pltpu.VMEM((n_hq, 1), F32)],
        ),
        compiler_params=_params("parallel", "arbitrary"),
    )(page_table, wq, bias, k_new, v_new, *([ck] * pps), *([cv] * pps))
    return out.reshape(n_seq * SAMPLE_ROWS, SB_WIDTH)


GLA_LEVELS = (64, 32, 16, 8, 4, 2, 1)


def _gla_static_matrix():
    idx = np.arange(BLK)
    mats = [(idx[None, :] <= idx[:, None]), (idx[None, :] > idx[:, None]), np.ones((BLK, BLK), bool)]
    for size in GLA_LEVELS:
        blk = idx // size
        odd = (blk % 2) == 1
        ref = np.where(odd, blk * size - 1, (blk + 1) * size - 1)
        j = idx[None, :]
        q_side = (j > ref[:, None]) & (j <= idx[:, None])
        k_side = (j > idx[:, None]) & (j <= ref[:, None])
        mats.append(np.where(odd[:, None], q_side, k_side))
    return np.concatenate(mats, axis=0).astype(np.float32)


def _gla_kernel(sm_ref, q_ref, k_ref, v_ref, g_ref, s0_ref, o_ref, sout_ref, state_ref):
    c = pl.program_id(2)

    @pl.when(c == 0)
    def _():
        state_ref[...] = s0_ref[...]

    g = g_ref[...]
    g_hi, g_lo = _split_bf16(g)
    sm = sm_ref[...]
    expo = (jnp.dot(sm, g_hi, preferred_element_type=F32) + jnp.dot(sm, g_lo, preferred_element_type=F32))
    q = q_ref[...] * (GLA_DK ** -0.5)
    k = k_ref[...]
    v = v_ref[...].astype(BF16)
    s_prev = state_ref[...]
    b = expo[0:BLK]
    rem = expo[BLK:2 * BLK]
    total = expo[2 * BLK:3 * BLK]

    row = lax.broadcasted_iota(jnp.int32, (BLK, BLK), 0)
    col = lax.broadcasted_iota(jnp.int32, (BLK, BLK), 1)
    nt = (((1,), (1,)), ((), ()))
    scores = jnp.where(row == col, jnp.sum(q * k, axis=1, keepdims=True), 0.0)
    for li, size in enumerate(GLA_LEVELS):
        f = jnp.exp(expo[(3 + li) * BLK:(4 + li) * BLK])
        part = lax.dot_general((q * f).astype(BF16), (k * f).astype(BF16), nt, preferred_element_type=F32)
        shift = size.bit_length() - 1
        rb = row >> shift
        pair = jnp.logical_and((rb & 1) == 1, (col >> shift) == rb - 1)
        scores = jnp.where(pair, part, scores)

    o = (jnp.dot((q * jnp.exp(b)).astype(BF16), s_prev.astype(BF16), preferred_element_type=F32)
         + jnp.dot(scores.astype(BF16), v, preferred_element_type=F32))
    o_ref[...] = o
    k_dec_t = jnp.transpose(k * jnp.exp(rem)).astype(BF16)
    kv = jnp.dot(k_dec_t, v, preferred_element_type=F32)
    decay = jnp.exp(jnp.transpose(total)[:, 0:1])
    s_new = decay * s_prev + kv
    state_ref[...] = s_new

    @pl.when(c == pl.num_programs(2) - 1)
    def _():
        sout_ref[...] = s_new


def gla(q_arr, k_arr, v_arr, g_arr, q_col, k_col, v_col, s0, n_batch, n_chunks, m_out):
    sm = jnp.asarray(_gla_static_matrix(), dtype=BF16)
    qc, kc, vc = q_col // GLA_DK, k_col // GLA_DK, v_col // GLA_DV
    return pl.pallas_call(
        _gla_kernel,
        out_shape=(jax.ShapeDtypeStruct((m_out, GLA_VW), F32),
                   jax.ShapeDtypeStruct((n_batch, GLA_HEADS, GLA_DK, GLA_DV), F32)),
        grid=(n_batch, GLA_HEADS, n_chunks),
        in_specs=[pl.BlockSpec(sm.shape, lambda b, h, c: (0, 0)),
                  pl.BlockSpec((BLK, GLA_DK), lambda b, h, c: (b * n_chunks + c, qc + h)),
                  pl.BlockSpec((BLK, GLA_DK), lambda b, h, c: (b * n_chunks + c, kc + h)),
                  pl.BlockSpec((BLK, GLA_DV), lambda b, h, c: (b * n_chunks + c, vc + h)),
                  pl.BlockSpec((BLK, GLA_DK), lambda b, h, c: (b * n_chunks + c, h)),
                  pl.BlockSpec((None, None, GLA_DK, GLA_DV), lambda b, h, c: (b, h, 0, 0))],
        out_specs=(pl.BlockSpec((BLK, GLA_DV), lambda b, h, c: (b * n_chunks + c, h)),
                   pl.BlockSpec((None, None, GLA_DK, GLA_DV), lambda b, h, c: (b, h, 0, 0))),
        scratch_shapes=[pltpu.VMEM((GLA_DK, GLA_DV), F32)],
        compiler_params=_params("parallel", "parallel", "arbitrary"),
    )(sm, q_arr, k_arr, v_arr, g_arr, s0)


def _merge_kernel(oa_ref, ob_ref, rb_ref, gn_ref, ga_ref, gb_ref, wa_ref, wb_ref, o_ref, obn_ref):
    @pl.when(pl.program_id(1) == 0)
    def _():
        for h in range(GLA_HEADS):
            sl = slice(h * GLA_DV, (h + 1) * GLA_DV)
            x = ob_ref[:, sl]
            y = x * lax.rsqrt(jnp.mean(x * x, axis=-1, keepdims=True) + EPS) * gn_ref[...]
            r = rb_ref[:, sl]
            obn_ref[:, sl] = (y * (r * _sigmoid(r))).astype(BF16)

    a = jnp.dot(oa_ref[...], wa_ref[...].astype(BF16), preferred_element_type=F32)
    b = jnp.dot(obn_ref[...], wb_ref[...].astype(BF16), preferred_element_type=F32)
    o_ref[...] = (_sigmoid(ga_ref[...]) * a + _sigmoid(gb_ref[...]) * b).astype(o_ref.dtype)


def merge_mix(o_a, o_b, z_main, z_gates, gla_norm_g, w_a_up, w_b_up, layer, d_model):
    m = o_a.shape[0]
    tm = _pick(m, (896, 512, 256, 128))
    tn = _pick(d_model, (512, 256, 128))
    nb = d_model // tn
    rb_blk = COL_RB // GLA_VW
    return pl.pallas_call(
        _merge_kernel,
        out_shape=jax.ShapeDtypeStruct((m, d_model), BF16),
        grid=(m // tm, nb),
        in_specs=[pl.BlockSpec((tm, SB_WIDTH), lambda i, j: (i, 0)),
                  pl.BlockSpec((tm, GLA_VW), lambda i, j: (i, 0)),
                  pl.BlockSpec((tm, GLA_VW), lambda i, j: (i, rb_blk)),
                  pl.BlockSpec((1, GLA_DV), lambda i, j: (0, 0)),
                  pl.BlockSpec((tm, tn), lambda i, j: (i, j)),
                  pl.BlockSpec((tm, tn), lambda i, j: (i, nb + j)),
                  pl.BlockSpec((None, SB_WIDTH, tn), lambda i, j: (layer, 0, j)),
                  pl.BlockSpec((None, GLA_VW, tn), lambda i, j: (layer, 0, j))],
        out_specs=pl.BlockSpec((tm, tn), lambda i, j: (i, j)),
        scratch_shapes=[pltpu.VMEM((tm, GLA_VW), BF16)],
        compiler_params=_params("parallel", "arbitrary"),
    )(o_a, o_b, z_main, gla_norm_g.reshape(1, GLA_DV), z_gates, z_gates, w_a_up, w_b_up)


def _router_kernel(h_ref, g_ref, wr_ref, br_ref, xn_ref, route_ref):
    x = h_ref[...]
    t = x * lax.rsqrt(jnp.mean(x * x, axis=-1, keepdims=True) + EPS) * g_ref[...]
    xn_ref[...] = t.astype(xn_ref.dtype)
    logits = jnp.dot(t, wr_ref[...], precision=lax.Precision.HIGHEST, preferred_element_type=F32) + br_ref[...]
    lane = lax.broadcasted_iota(jnp.int32, logits.shape, 1)
    is_group = lane < N_GROUPS
    gl = jnp.where(is_group, logits, NEG_BIG)
    g_max = jnp.max(gl, axis=1, keepdims=True)
    g_idx = jnp.min(jnp.where(gl == g_max, lane, BLK), axis=1, keepdims=True)
    g_sum = jnp.sum(jnp.where(is_group, jnp.exp(gl - g_max), 0.0), axis=1, keepdims=True)
    g_w = 1.0 / g_sum
    lo = N_GROUPS + EXPERTS_PER_GROUP * g_idx
    in_grp = jnp.logical_and(lane >= lo, lane < lo + EXPERTS_PER_GROUP)
    e1 = jnp.where(in_grp, logits, NEG_BIG)
    v1 = jnp.max(e1, axis=1, keepdims=True)
    i1 = jnp.min(jnp.where(e1 == v1, lane, BLK), axis=1, keepdims=True)
    e2 = jnp.where(lane == i1, NEG_BIG, e1)
    v2 = jnp.max(e2, axis=1, keepdims=True)
    i2 = jnp.min(jnp.where(jnp.logical_and(e2 == v2, in_grp), lane, BLK), axis=1, keepdims=True)
    r = jnp.exp(v2 - v1)
    w1 = g_w / (1.0 + r)
    w2 = w1 * r
    route = jnp.where(lane == 0, (i1 - N_GROUPS).astype(F32),
                      jnp.where(lane == 1, (i2 - N_GROUPS).astype(F32),
                                jnp.where(lane == 2, w1, jnp.where(lane == 3, w2, 0.0))))
    route_ref[...] = route


def router(h, norm_g, w_r, b_r):
    m, d = h.shape
    tm = _pick(m, (256, 128, 64, 16))
    return pl.pallas_call(
        _router_kernel,
        out_shape=(jax.ShapeDtypeStruct((m, d), BF16), jax.ShapeDtypeStruct((m, BLK), F32)),
        grid=(m // tm,),
        in_specs=[pl.BlockSpec((tm, d), lambda i: (i, 0)),
                  pl.BlockSpec((1, d), lambda i: (0, 0)),
                  pl.BlockSpec((d, BLK), lambda i: (0, 0)),
                  pl.BlockSpec((1, BLK), lambda i: (0, 0))],
        out_specs=(pl.BlockSpec((tm, d), lambda i: (i, 0)), pl.BlockSpec((tm, BLK), lambda i: (i, 0))),
        compiler_params=_params("parallel"),
    )(h, norm_g.reshape(1, d), w_r, b_r)


def _expert_kernel(te_ref, nu_ref, x_ref, sw_ref, wg_ref, wu_ref, wd_ref, o_ref):
    i = pl.program_id(0)

    @pl.when(i < nu_ref[0])
    def _():
        x = x_ref[...]
        gate = jnp.dot(x, wg_ref[...].astype(BF16), preferred_element_type=F32)
        up = jnp.dot(x, wu_ref[...].astype(BF16), preferred_element_type=F32)
        hid = (gate * _sigmoid(gate)) * up
        hid = (hid * sw_ref[...]).astype(BF16)
        o_ref[...] = jnp.dot(hid, wd_ref[...].astype(BF16), preferred_element_type=F32)

    @pl.when(i >= nu_ref[0])
    def _():
        o_ref[...] = jnp.zeros_like(o_ref)


def expert_mlp(xg, slot_w, tile_expert, n_used, w_g, w_u, w_d, layer):
    r, d = xg.shape
    nt = r // EXPERT_TILE

    def x_map(i, te, nu):
        return (jnp.minimum(i, nu[0] - 1), 0)

    return pl.pallas_call(
        _expert_kernel,
        out_shape=jax.ShapeDtypeStruct((r, d), F32),
        grid_spec=pltpu.PrefetchScalarGridSpec(
            num_scalar_prefetch=2,
            grid=(nt,),
            in_specs=[pl.BlockSpec((EXPERT_TILE, d), x_map),
                      pl.BlockSpec((EXPERT_TILE, 1), x_map),
                      pl.BlockSpec((None, None, d, D_EXPERT), lambda i, te, nu: (layer, te[i], 0, 0)),
                      pl.BlockSpec((None, None, d, D_EXPERT), lambda i, te, nu: (layer, te[i], 0, 0)),
                      pl.BlockSpec((None, None, D_EXPERT, d), lambda i, te, nu: (layer, te[i], 0, 0))],
            out_specs=pl.BlockSpec((EXPERT_TILE, d), lambda i, te, nu: (i, 0)),
        ),
        compiler_params=_params("arbitrary"),
    )(tile_expert, n_used, xg, slot_w, w_g, w_u, w_d)


def moe_dispatch(route, m):
    e_flat = route[:, 0:2].astype(jnp.int32).reshape(-1)
    w_flat = route[:, 2:4].reshape(-1)
    n_slots = 2 * m
    n_rows = n_slots + N_EXPERTS * EXPERT_TILE
    order = jnp.argsort(e_flat, stable=True).astype(jnp.int32)
    e_sorted = e_flat[order]
    counts = jnp.zeros((N_EXPERTS,), jnp.int32).at[e_flat].add(1)
    padded = ((counts + EXPERT_TILE - 1) // EXPERT_TILE) * EXPERT_TILE
    ends_p = jnp.cumsum(padded)
    starts_p = ends_p - padded
    starts = jnp.cumsum(counts) - counts
    pos_sorted = starts_p[e_sorted] + (jnp.arange(n_slots, dtype=jnp.int32) - starts[e_sorted])
    src_token = jnp.zeros((n_rows,), jnp.int32).at[pos_sorted].set(order // 2)
    slot_w = jnp.zeros((n_rows,), F32).at[pos_sorted].set(w_flat[order])
    pos = jnp.zeros((n_slots,), jnp.int32).at[order].set(pos_sorted).reshape(m, 2)
    n_used = (ends_p[-1] // EXPERT_TILE).astype(jnp.int32)
    tile_start = jnp.arange(n_rows // EXPERT_TILE, dtype=jnp.int32) * EXPERT_TILE
    tile_expert = jnp.searchsorted(ends_p, jnp.minimum(tile_start, ends_p[-1] - 1), side="right")
    tile_expert = jnp.minimum(tile_expert, N_EXPERTS - 1).astype(jnp.int32)
    return src_token, slot_w.reshape(n_rows, 1), pos, tile_expert, n_used.reshape(1)


def kernel(x_prompt, x_sample, cache_k, cache_v, state_gla, page_table, meta, norm1_g, w_in, sb_bias, gla_w_a2, gla_b_a, gla_norm_g, w_a_up, w_b_up, w_out, norm2_g, w_router_group, b_router_group, w_router_expert, b_router_expert, w_exp_gate, w_exp_up, w_exp_down, final_norm_g):
    n_b, seq, d = x_prompt.shape
    n_s, n_t, _ = x_sample.shape
    depth = w_in.shape[0]
    assert seq % BLK == 0 and n_t <= SAMPLE_ROWS
    rows_per_seq = seq + BLK
    n_chunks = rows_per_seq // BLK
    m_prompt = n_b * rows_per_seq
    m = m_prompt + n_s * SAMPLE_ROWS

    meta_rows = jnp.broadcast_to(meta.astype(F32)[None], (n_b, N_META, d))
    h_p = jnp.concatenate([jnp.zeros((n_b, FRONT_PAD, d), F32), meta_rows, x_prompt], axis=1)
    h_s = jnp.pad(x_sample, ((0, 0), (0, SAMPLE_ROWS - n_t), (0, 0)))
    h = jnp.concatenate([h_p.reshape(m_prompt, d), h_s.reshape(n_s * SAMPLE_ROWS, d)], axis=0)

    zero_state = jnp.zeros((n_b, GLA_HEADS, GLA_DK, GLA_DV), F32)
    kp_rows, vp_rows, ks_rows, vs_rows, sp_list, ss_list = [], [], [], [], [], []
    for l in range(depth):
        xn = rms_norm_rows(h, norm1_g[l], BF16)
        z_main = matmul(xn, w_in, N_MAIN, layer=l)
        w_gates = w_in[l][:, COL_GATES:].astype(BF16)
        z_gates = matmul(xn, w_gates, 2 * d)
        w_ab = jnp.pad(w_in[l][:, COL_AB:COL_AB + GLA_RANK], ((0, 0), (0, BLK - GLA_RANK)))
        w_a2 = jnp.pad(gla_w_a2[l], ((0, BLK - GLA_RANK), (0, 0)))
        lg = forget_gate(xn, w_ab, w_a2, gla_b_a[l])

        z_s = z_main[m_prompt:].reshape(n_s, SAMPLE_ROWS, N_MAIN)[:, :n_t]
        ks_new = z_s[:, :, COL_KA:COL_KA + SB_WIDTH]
        vs_new = z_s[:, :, COL_VA:COL_VA + SB_WIDTH]

        o_a = sb_prompt(z_main, sb_bias[l], n_b, rows_per_seq, m)
        q_s = z_s[:, :, COL_QA:COL_QA + SB_WIDTH].reshape(n_s, n_t, SB_HEADS, SB_HEAD_DIM)
        o_a_s = sb_paged(q_s, ks_new, vs_new, cache_k, cache_v, page_table, sb_bias[l], l)
        o_a = lax.dynamic_update_slice(o_a, o_a_s, (m_prompt, 0))

        o_b, s_p = gla(z_main, z_main, z_main, lg, COL_QB, COL_KB, COL_VB, zero_state, n_b, n_chunks, m)
        pad_s = lambda a: jnp.pad(a, ((0, 0), (0, BLK - n_t), (0, 0))).reshape(n_s * BLK, a.shape[-1])
        lg_s = lg[m_prompt:].reshape(n_s, SAMPLE_ROWS, GLA_KW)[:, :n_t]
        o_b_s, s_s = gla(pad_s(z_s[:, :, COL_QB:COL_KB]), pad_s(z_s[:, :, COL_KB:COL_VB]),
                         pad_s(z_s[:, :, COL_VB:COL_RB]), pad_s(lg_s), 0, 0, 0,
                         state_gla[l], n_s, 1, n_s * BLK)
        o_b_s = o_b_s.reshape(n_s, BLK, GLA_VW)[:, :SAMPLE_ROWS].reshape(n_s * SAMPLE_ROWS, GLA_VW)
        o_b = lax.dynamic_update_slice(o_b, o_b_s, (m_prompt, 0))

        mix = merge_mix(o_a, o_b, z_main, z_gates, gla_norm_g[l], w_a_up, w_b_up, l, d)
        h = matmul(mix, w_out, d, residual=h, layer=l)

        w_r = jnp.pad(jnp.concatenate([w_router_group[l], w_router_expert[l]], axis=1),
                      ((0, 0), (0, BLK - N_GROUPS - N_EXPERTS)))
        b_r = jnp.pad(jnp.concatenate([b_router_group[l], b_router_expert[l]]),
                      (0, BLK - N_GROUPS - N_EXPERTS)).reshape(1, BLK)
        xn2, route = router(h, norm2_g[l], w_r, b_r)
        src_token, slot_w, pos, tile_expert, n_used = moe_dispatch(route, m)
        xg = jnp.take(xn2, src_token, axis=0)
        y_slots = expert_mlp(xg, slot_w, tile_expert, n_used, w_exp_gate, w_exp_up, w_exp_down, l)
        h = h + jnp.take(y_slots, pos[:, 0], axis=0) + jnp.take(y_slots, pos[:, 1], axis=0)

        kv_p = z_main[:m_prompt].reshape(n_b, rows_per_seq, N_MAIN)[:, FRONT_PAD:]
        kp_rows.append(kv_p[:, :, COL_KA:COL_KA + SB_WIDTH].reshape(n_b, -1, SB_HEADS, SB_HEAD_DIM))
        vp_rows.append(kv_p[:, :, COL_VA:COL_VA + SB_WIDTH].reshape(n_b, -1, SB_HEADS, SB_HEAD_DIM))
        ks_rows.append(ks_new.reshape(n_s, n_t, SB_HEADS, SB_HEAD_DIM))
        vs_rows.append(vs_new.reshape(n_s, n_t, SB_HEADS, SB_HEAD_DIM))
        sp_list.append(s_p)
        ss_list.append(s_s)

    yn = rms_norm_rows(h, final_norm_g, F32)
    y_prompt = yn[:m_prompt].reshape(n_b, rows_per_seq, d)[:, BLK:]
    y_sample = yn[m_prompt:].reshape(n_s, SAMPLE_ROWS, d)[:, :n_t]
    return (y_prompt, y_sample, jnp.stack(kp_rows), jnp.stack(vp_rows), jnp.stack(ks_rows),
            jnp.stack(vs_rows), jnp.stack(sp_list), jnp.stack(ss_list))
```

```python
import functools

import numpy as np
import jax
import jax.numpy as jnp
from jax import lax
from jax.experimental import pallas as pl
from jax.experimental.pallas import tpu as pltpu

F32 = jnp.float32
BF16 = jnp.bfloat16

N_META = 16
SB_HEADS = 8
SB_HEAD_DIM = 128
SB_WIDTH = SB_HEADS * SB_HEAD_DIM
GLA_HEADS = 4
GLA_DK = 128
GLA_DV = 256
GLA_KW = GLA_HEADS * GLA_DK
GLA_VW = GLA_HEADS * GLA_DV
GLA_RANK = 16
GLA_TAU = 16.0
N_GROUPS = 4
EXPERTS_PER_GROUP = 8
N_EXPERTS = N_GROUPS * EXPERTS_PER_GROUP
D_EXPERT = 256
EPS = 1e-6

BLK = 128
FRONT_PAD = BLK - N_META
SAMPLE_ROWS = 8
COL_QA, COL_KA, COL_VA = 0, SB_WIDTH, 2 * SB_WIDTH
COL_QB = 3 * SB_WIDTH
COL_KB = COL_QB + GLA_KW
COL_VB = COL_KB + GLA_KW
COL_RB = COL_VB + GLA_VW
N_MAIN = COL_RB + GLA_VW
COL_AB = N_MAIN
COL_GATES = N_MAIN + GLA_RANK

EXPERT_TILE = 256
VMEM_LIMIT = 48 * 1024 * 1024
NEG_BIG = -1e30


def _params(*sem):
    return pltpu.CompilerParams(dimension_semantics=sem, vmem_limit_bytes=VMEM_LIMIT)


def _pick(m, prefs):
    for p in prefs:
        if m % p == 0:
            return p
    return m


def _softplus(z):
    return jnp.maximum(z, 0.0) + jnp.log1p(jnp.exp(-jnp.abs(z)))


def _sigmoid(z):
    return 1.0 / (1.0 + jnp.exp(-z))


def _split_bf16(x):
    hi = x.astype(BF16)
    lo = (x - hi.astype(F32)).astype(BF16)
    return hi, lo


def _rms_kernel(x_ref, g_ref, o_ref):
    x = x_ref[...]
    ms = jnp.mean(x * x, axis=-1, keepdims=True)
    o_ref[...] = (x * lax.rsqrt(ms + EPS) * g_ref[...]).astype(o_ref.dtype)


def rms_norm_rows(x, g, out_dtype, row0=0, n_rows=None):
    d = x.shape[1]
    m = x.shape[0] - row0 if n_rows is None else n_rows
    tm = _pick(np.gcd(m, row0) if row0 else m, (256, 128, 64, 16))
    off = row0 // tm
    return pl.pallas_call(
        _rms_kernel,
        out_shape=jax.ShapeDtypeStruct((m, d), out_dtype),
        grid=(m // tm,),
        in_specs=[pl.BlockSpec((tm, d), lambda i: (i + off, 0)), pl.BlockSpec((1, d), lambda i: (0, 0))],
        out_specs=pl.BlockSpec((tm, d), lambda i: (i, 0)),
        compiler_params=_params("parallel"),
    )(x, g.reshape(1, d))


def _dot(a, b, precise, dims=None):
    dt, prec = (F32, lax.Precision.HIGHEST) if precise else (BF16, None)
    a, b = a.astype(dt), b.astype(dt)
    if dims is None:
        return jnp.dot(a, b, precision=prec, preferred_element_type=F32)
    return lax.dot_general(a, b, dims, precision=prec, preferred_element_type=F32)


def _mm_kernel(precise, x_ref, w_ref, o_ref):
    o_ref[...] = _dot(x_ref[...], w_ref[...], precise).astype(o_ref.dtype)


def _mm_res_kernel(precise, x_ref, w_ref, r_ref, o_ref):
    o_ref[...] = r_ref[...] + _dot(x_ref[...], w_ref[...], precise)


def matmul(x, w, n_out, out_dtype=F32, residual=None, row0=0, layer=None, precise=False):
    m, k = x.shape
    tm = _pick(int(np.gcd(m, row0)) if row0 else m, (2176, 1792, 1088, 1024, 896, 512, 256, 128))
    tn = _pick(n_out, (512, 256, 128))
    off = row0 // tm
    if layer is None:
        w_spec = pl.BlockSpec((k, tn), lambda i, j: (0, j))
    else:
        w_spec = pl.BlockSpec((None, k, tn), lambda i, j: (layer, 0, j))
    in_specs = [pl.BlockSpec((tm, k), lambda i, j: (i, 0)), w_spec]
    args = [x, w]
    body = _mm_kernel
    out_shape = jax.ShapeDtypeStruct((m, n_out), out_dtype)
    aliases = {}
    if residual is not None:
        in_specs.append(pl.BlockSpec((tm, tn), lambda i, j: (i + off, j)))
        args.append(residual)
        body = _mm_res_kernel
        out_shape = jax.ShapeDtypeStruct(residual.shape, residual.dtype)
        aliases = {2: 0}
    return pl.pallas_call(
        functools.partial(body, precise),
        out_shape=out_shape,
        grid=(m // tm, n_out // tn),
        in_specs=in_specs,
        out_specs=pl.BlockSpec((tm, tn), lambda i, j: (i + off, j)),
        input_output_aliases=aliases,
        compiler_params=_params("parallel", "parallel"),
    )(*args)


def _lg_kernel(precise, x_ref, wab_ref, wa2_ref, ba_ref, o_ref):
    ab = _dot(x_ref[...], wab_ref[...], precise)
    z = _dot(ab, wa2_ref[...], precise) + ba_ref[...]
    o_ref[...] = -_softplus(-z) * (1.0 / GLA_TAU)


def forget_gate(xn, w_ab, w_a2, b_a, precise=False):
    m, d = xn.shape
    tm = _pick(m, (1088, 896, 512, 256, 128))
    return pl.pallas_call(
        functools.partial(_lg_kernel, precise),
        out_shape=jax.ShapeDtypeStruct((m, GLA_KW), F32),
        grid=(m // tm,),
        in_specs=[pl.BlockSpec((tm, d), lambda i: (i, 0)),
                  pl.BlockSpec((d, BLK), lambda i: (0, 0)),
                  pl.BlockSpec((BLK, GLA_KW), lambda i: (0, 0)),
                  pl.BlockSpec((1, GLA_KW), lambda i: (0, 0))],
        out_specs=pl.BlockSpec((tm, GLA_KW), lambda i: (i, 0)),
        compiler_params=_params("parallel"),
    )(xn, w_ab, w_a2, b_a.reshape(1, GLA_KW))


SB_HEAD_GROUP = 4


def _later_and_total_matrix():
    row = lax.broadcasted_iota(jnp.int32, (BLK, 2 * BLK), 0)
    col = lax.broadcasted_iota(jnp.int32, (BLK, 2 * BLK), 1)
    return jnp.where(jnp.logical_or(row > col, col >= BLK), 1.0, 0.0).astype(BF16)


def _sb_prompt_kernel(bias_ref, q_ref, k_ref, v_ref, o_ref, acc_ref, run_ref):
    hg = pl.program_id(1)
    qi = pl.program_id(2)
    row = lax.broadcasted_iota(jnp.int32, (BLK, BLK), 0)
    col = lax.broadcasted_iota(jnp.int32, (BLK, BLK), 1)
    sum_mat = _later_and_total_matrix()
    q_pos = qi * BLK + row
    scale = SB_HEAD_DIM ** -0.5
    acc_ref[...] = jnp.zeros_like(acc_ref)
    run_ref[...] = jnp.zeros_like(run_ref)

    def block(kb, masked):
        off = pl.multiple_of(kb * BLK, BLK)
        if masked:
            k_pos = kb * BLK + col
            mask = jnp.logical_and(k_pos < q_pos, k_pos >= FRONT_PAD)
        heads = range(SB_HEAD_GROUP)
        sls = [slice(h * SB_HEAD_DIM, (h + 1) * SB_HEAD_DIM) for h in heads]
        qs = [q_ref[:, sl].astype(BF16) for sl in sls]
        ks = [k_ref[pl.ds(off, BLK), sl].astype(BF16) for sl in sls]
        vs = [v_ref[pl.ds(off, BLK), sl].astype(BF16) for sl in sls]
        runs = [run_ref[h] for h in heads]
        accs = [acc_ref[h] for h in heads]
        nt = (((1,), (1,)), ((), ()))
        zs = [lax.dot_general(qs[h], ks[h], nt, preferred_element_type=F32) * scale
              + bias_ref[hg * SB_HEAD_GROUP + h] for h in heads]
        sps = [jnp.maximum(z, 0.0) + jnp.log(1.0 + jnp.exp(-jnp.abs(z))) for z in zs]
        stays = [jnp.where(mask, -sp, 0.0) if masked else -sp for sp in sps]
        splits = [_split_bf16(s) for s in stays]
        sums = [jnp.dot(hi, sum_mat, preferred_element_type=F32) + jnp.dot(lo, sum_mat, preferred_element_type=F32)
                for hi, lo in splits]
        ws = [jnp.exp(zs[h] - sps[h] + runs[h] + sums[h][:, :BLK]) for h in heads]
        if masked:
            ws = [jnp.where(mask, w, 0.0) for w in ws]
        pvs = [jnp.dot(ws[h].astype(BF16), vs[h], preferred_element_type=F32) for h in heads]
        for h in heads:
            acc_ref[h] = accs[h] + pvs[h]
            run_ref[h] = runs[h] + sums[h][:, BLK:]

    block(qi, True)

    def body(i, carry):
        block(qi - i, False)
        return carry

    lax.fori_loop(1, qi, body, 0)

    @pl.when(qi > 0)
    def _():
        block(0, True)

    for h in range(SB_HEAD_GROUP):
        o_ref[:, h * SB_HEAD_DIM:(h + 1) * SB_HEAD_DIM] = acc_ref[h].astype(o_ref.dtype)


def sb_prompt(z_main, sb_bias, n_batch, rows_per_seq, m_total):
    nq = rows_per_seq // BLK
    gw = SB_HEAD_GROUP * SB_HEAD_DIM
    n_groups = SB_HEADS // SB_HEAD_GROUP
    kcol = COL_KA // gw
    vcol = COL_VA // gw
    return pl.pallas_call(
        _sb_prompt_kernel,
        out_shape=jax.ShapeDtypeStruct((m_total, SB_WIDTH), BF16),
        grid=(n_batch, n_groups, nq),
        in_specs=[pl.BlockSpec(memory_space=pltpu.SMEM),
                  pl.BlockSpec((BLK, gw), lambda b, g, i: (b * nq + i, g)),
                  pl.BlockSpec((rows_per_seq, gw), lambda b, g, i: (b, kcol + g)),
                  pl.BlockSpec((rows_per_seq, gw), lambda b, g, i: (b, vcol + g))],
        out_specs=pl.BlockSpec((BLK, gw), lambda b, g, i: (b * nq + i, g)),
        scratch_shapes=[pltpu.VMEM((SB_HEAD_GROUP, BLK, SB_HEAD_DIM), F32),
                        pltpu.VMEM((SB_HEAD_GROUP, BLK, BLK), F32)],
        compiler_params=_params("parallel", "parallel", "arbitrary"),
    )(sb_bias, z_main, z_main, z_main)


Q_ROWS = 16


PAGE_GROUP = 2


def _later_matrix(n_pages):
    idx = np.arange(n_pages * BLK)
    pj, ps = idx[:, None] // BLK, idx[None, :] // BLK
    later = (pj < ps) | ((pj == ps) & (idx[:, None] > idx[None, :]))
    return later.astype(np.float32)


def _sb_paged_kernel(pages_per_step, n_t, pt_ref, q_ref, bias_ref, lat1_ref, latg_ref, knew_ref, vnew_ref, *rest):
    k_refs = rest[:pages_per_step]
    v_refs = rest[pages_per_step:2 * pages_per_step]
    o_ref, acc_ref, run_ref = rest[2 * pages_per_step:]
    p = pl.program_id(1)
    n_rows = SB_HEADS * Q_ROWS
    scale = SB_HEAD_DIM ** -0.5
    nt = (((1,), (1,)), ((), ()))
    heads = range(SB_HEADS)

    def head_rows(refs, h):
        return jnp.concatenate([r[pl.ds(h, BLK, stride=SB_HEADS), :] for r in refs], axis=0).astype(BF16)

    def process(groups, later_ref, mask, run, accs):
        gs = range(len(groups))
        ks = [[head_rows(kl, h) for h in heads] for kl, _ in groups]
        vs = [[head_rows(vl, h) for h in heads] for _, vl in groups]
        bias = jnp.concatenate([bias_ref[...]] * len(groups[0][0]), axis=1)
        zs = [jnp.concatenate([lax.dot_general(q_ref[h], ks[g][h], nt, preferred_element_type=F32) for h in heads],
                              axis=0) * scale + bias for g in gs]
        sps = [jnp.maximum(z, 0.0) + jnp.log(1.0 + jnp.exp(-jnp.abs(z))) for z in zs]
        stays = [-sp if mask is None else jnp.where(mask, -sp, 0.0) for sp in sps]
        later = later_ref[...]
        sums = []
        for s in stays:
            hi, lo = _split_bf16(s)
            both = jnp.dot(jnp.concatenate([hi, lo], axis=0), later, preferred_element_type=F32)
            sums.append(both[:n_rows] + both[n_rows:])
        totals = [jnp.sum(s, axis=1, keepdims=True) for s in stays]
        ws = []
        for g in gs:
            w = jnp.exp(zs[g] - sps[g] + run + sums[g])
            ws.append((w if mask is None else jnp.where(mask, w, 0.0)).astype(BF16))
            run = run + totals[g]
        for g in gs:
            accs = [accs[h] + jnp.dot(ws[g][h * Q_ROWS:(h + 1) * Q_ROWS], vs[g][h], preferred_element_type=F32)
                    for h in heads]
        return run, accs

    @pl.when(p == 0)
    def _():
        t = lax.broadcasted_iota(jnp.int32, (n_rows, BLK), 0) & (Q_ROWS - 1)
        key = lax.broadcasted_iota(jnp.int32, (n_rows, BLK), 1)
        zero_acc = [jnp.zeros((Q_ROWS, SB_HEAD_DIM), F32) for _ in heads]
        run, accs = process([([knew_ref], [vnew_ref])], lat1_ref, key < t,
                            jnp.zeros((n_rows, 1), F32), zero_acc)
        run_ref[...] = run
        for h in heads:
            acc_ref[h] = accs[h]

    run = run_ref[...]
    accs = [acc_ref[h] for h in heads]
    groups = [(k_refs[g:g + PAGE_GROUP], v_refs[g:g + PAGE_GROUP]) for g in range(0, pages_per_step, PAGE_GROUP)]
    run, accs = process(groups, latg_ref, None, run, accs)
    run_ref[...] = run
    for h in heads:
        acc_ref[h] = accs[h]

    @pl.when(p == pl.num_programs(1) - 1)
    def _():
        valid = lax.broadcasted_iota(jnp.int32, (SAMPLE_ROWS, SB_HEAD_DIM), 0) < n_t
        for h in heads:
            o_ref[:, h * SB_HEAD_DIM:(h + 1) * SB_HEAD_DIM] = jnp.where(valid, accs[h][:SAMPLE_ROWS], 0.0)


def sb_paged(q_s, k_new, v_new, cache_k, cache_v, page_table, sb_bias, layer):
    n_seq, n_t = q_s.shape[0], q_s.shape[1]
    n_pages = page_table.shape[1]
    page = cache_k.shape[2]
    assert page == BLK and n_t <= SAMPLE_ROWS <= Q_ROWS and n_pages % PAGE_GROUP == 0
    pps = 2 * PAGE_GROUP if n_pages % (2 * PAGE_GROUP) == 0 else PAGE_GROUP
    later_1 = jnp.asarray(_later_matrix(1), dtype=BF16)
    later_g = jnp.asarray(_later_matrix(PAGE_GROUP), dtype=BF16)
    n_rows = SB_HEADS * Q_ROWS
    page_rows = page * SB_HEADS
    q = jnp.pad(jnp.transpose(q_s, (0, 2, 1, 3)), ((0, 0), (0, 0), (0, Q_ROWS - n_t), (0, 0))).astype(BF16)
    bias = jnp.broadcast_to(jnp.repeat(sb_bias.astype(F32), Q_ROWS)[:, None], (n_rows, BLK))
    pad = ((0, 0), (0, BLK - n_t), (0, 0))
    k_new = jnp.pad(k_new, pad).reshape(n_seq, page_rows, SB_HEAD_DIM)
    v_new = jnp.pad(v_new, pad).reshape(n_seq, page_rows, SB_HEAD_DIM)
    ck = cache_k.reshape(cache_k.shape[0], cache_k.shape[1], page_rows, SB_HEAD_DIM)
    cv = cache_v.reshape(cache_v.shape[0], cache_v.shape[1], page_rows, SB_HEAD_DIM)

    def page_spec(j):
        return pl.BlockSpec((None, None, page_rows, SB_HEAD_DIM),
                            lambda s, p, pt: (layer, pt[s, n_pages - 1 - (p * pps + j)], 0, 0))

    in_specs = ([pl.BlockSpec((None, SB_HEADS, Q_ROWS, SB_HEAD_DIM), lambda s, p, pt: (s, 0, 0, 0)),
                 pl.BlockSpec((n_rows, BLK), lambda s, p, pt: (0, 0)),
                 pl.BlockSpec(later_1.shape, lambda s, p, pt: (0, 0)),
                 pl.BlockSpec(later_g.shape, lambda s, p, pt: (0, 0)),
                 pl.BlockSpec((None, page_rows, SB_HEAD_DIM), lambda s, p, pt: (s, 0, 0)),
                 pl.BlockSpec((None, page_rows, SB_HEAD_DIM), lambda s, p, pt: (s, 0, 0))]
                + [page_spec(j) for j in range(pps)] + [page_spec(j) for j in range(pps)])
    out = pl.pallas_call(
        functools.partial(_sb_paged_kernel, pps, n_t),
        out_shape=jax.ShapeDtypeStruct((n_seq, SAMPLE_ROWS, SB_WIDTH), F32),
        grid_spec=pltpu.PrefetchScalarGridSpec(
            num_scalar_prefetch=1,
            grid=(n_seq, n_pages // pps),
            in_specs=in_specs,
            out_specs=pl.BlockSpec((None, SAMPLE_ROWS, SB_WIDTH), lambda s, p, pt: (s, 0, 0)),
            scratch_shapes=[pltpu.VMEM((SB_HEADS, Q_ROWS, SB_HEAD_DIM), F32), pltpu.VMEM((n_rows, 1), F32)],
        ),
        compiler_params=_params("parallel", "arbitrary"),
    )(page_table, q, bias, later_1, later_g, k_new, v_new, *([ck] * pps), *([cv] * pps))
    return out.reshape(n_seq * SAMPLE_ROWS, SB_WIDTH).astype(BF16)


GLA_LEVELS = (64, 32, 16, 8, 4, 2, 1)


def _gla_static_matrix(levels):
    idx = np.arange(BLK)
    mats = [(idx[None, :] <= idx[:, None]), (idx[None, :] > idx[:, None]), np.ones((BLK, BLK), bool)]
    for size in levels:
        blk = idx // size
        odd = (blk % 2) == 1
        ref = np.where(odd, blk * size - 1, (blk + 1) * size - 1)
        j = idx[None, :]
        q_side = (j > ref[:, None]) & (j <= idx[:, None])
        k_side = (j > idx[:, None]) & (j <= ref[:, None])
        mats.append(np.where(odd[:, None], q_side, k_side))
    return np.concatenate(mats, axis=0).astype(np.float32)


def _gla_kernel(levels, precise, sm_ref, q_ref, k_ref, v_ref, g_ref, s0_ref, o_ref, sout_ref, state_ref):
    c = pl.program_id(2)

    @pl.when(c == 0)
    def _():
        state_ref[...] = s0_ref[...]

    g = g_ref[...]
    g_hi, g_lo = _split_bf16(g)
    sm = sm_ref[...]
    expo = (jnp.dot(sm, g_hi, preferred_element_type=F32) + jnp.dot(sm, g_lo, preferred_element_type=F32))
    q = q_ref[...] * (GLA_DK ** -0.5)
    k = k_ref[...]
    v = v_ref[...]
    s_prev = state_ref[...]
    b = expo[0:BLK]
    rem = expo[BLK:2 * BLK]
    total = expo[2 * BLK:3 * BLK]

    row = lax.broadcasted_iota(jnp.int32, (BLK, BLK), 0)
    col = lax.broadcasted_iota(jnp.int32, (BLK, BLK), 1)
    nt = (((1,), (1,)), ((), ()))
    scores = jnp.where(row == col, jnp.sum(q * k, axis=1, keepdims=True), 0.0)
    for li, size in enumerate(levels):
        f = jnp.exp(expo[(3 + li) * BLK:(4 + li) * BLK])
        part = _dot(q * f, k * f, precise, nt)
        shift = size.bit_length() - 1
        rb = row >> shift
        pair = jnp.logical_and((rb & 1) == 1, (col >> shift) == rb - 1)
        scores = jnp.where(pair, part, scores)

    o_ref[...] = _dot(q * jnp.exp(b), s_prev, precise) + _dot(scores, v, precise)
    kv = _dot(jnp.transpose(k * jnp.exp(rem)), v, precise)
    decay = jnp.exp(jnp.transpose(total)[:, 0:1])
    s_new = decay * s_prev + kv
    state_ref[...] = s_new
    sout_ref[...] = s_new


def gla(q_arr, k_arr, v_arr, g_arr, q_col, k_col, v_col, s0, n_batch, n_chunks, m_out,
        levels=GLA_LEVELS, precise=False):
    sm = jnp.asarray(_gla_static_matrix(levels), dtype=BF16)
    qc, kc, vc = q_col // GLA_DK, k_col // GLA_DK, v_col // GLA_DV
    return pl.pallas_call(
        functools.partial(_gla_kernel, levels, precise),
        out_shape=(jax.ShapeDtypeStruct((m_out, GLA_VW), F32),
                   jax.ShapeDtypeStruct((n_batch, GLA_HEADS, GLA_DK, GLA_DV), F32)),
        grid=(n_batch, GLA_HEADS, n_chunks),
        in_specs=[pl.BlockSpec(sm.shape, lambda b, h, c: (0, 0)),
                  pl.BlockSpec((BLK, GLA_DK), lambda b, h, c: (b * n_chunks + c, qc + h)),
                  pl.BlockSpec((BLK, GLA_DK), lambda b, h, c: (b * n_chunks + c, kc + h)),
                  pl.BlockSpec((BLK, GLA_DV), lambda b, h, c: (b * n_chunks + c, vc + h)),
                  pl.BlockSpec((BLK, GLA_DK), lambda b, h, c: (b * n_chunks + c, h)),
                  pl.BlockSpec((None, None, GLA_DK, GLA_DV), lambda b, h, c: (b, h, 0, 0))],
        out_specs=(pl.BlockSpec((BLK, GLA_DV), lambda b, h, c: (b * n_chunks + c, h)),
                   pl.BlockSpec((None, None, GLA_DK, GLA_DV), lambda b, h, c: (b, h, 0, 0))),
        scratch_shapes=[pltpu.VMEM((GLA_DK, GLA_DV), F32)],
        compiler_params=_params("parallel", "parallel", "arbitrary"),
    )(sm, q_arr, k_arr, v_arr, g_arr, s0)


def _merge_kernel(precise, oa_ref, ob_ref, rb_ref, gn_ref, ga_ref, gb_ref, wa_ref, wb_ref, o_ref, obn_ref):
    @pl.when(pl.program_id(1) == 0)
    def _():
        for h in range(GLA_HEADS):
            sl = slice(h * GLA_DV, (h + 1) * GLA_DV)
            x = ob_ref[:, sl]
            y = x * lax.rsqrt(jnp.mean(x * x, axis=-1, keepdims=True) + EPS) * gn_ref[...]
            r = rb_ref[:, sl]
            obn_ref[:, sl] = (y * (r * _sigmoid(r))).astype(obn_ref.dtype)

    a = _dot(oa_ref[...], wa_ref[...], precise)
    b = _dot(obn_ref[...], wb_ref[...], precise)
    o_ref[...] = (_sigmoid(ga_ref[...]) * a + _sigmoid(gb_ref[...]) * b).astype(o_ref.dtype)


def merge_mix(o_a, o_b, z_main, z_gates, gla_norm_g, w_a_up, w_b_up, layer, d_model, precise=False):
    m = o_a.shape[0]
    tm = _pick(m, (1088, 896, 512, 256, 128))
    tn = _pick(d_model, (512, 256, 128))
    nb = d_model // tn
    rb_blk = COL_RB // GLA_VW
    act = F32 if precise else BF16
    return pl.pallas_call(
        functools.partial(_merge_kernel, precise),
        out_shape=jax.ShapeDtypeStruct((m, d_model), act),
        grid=(m // tm, nb),
        in_specs=[pl.BlockSpec((tm, SB_WIDTH), lambda i, j: (i, 0)),
                  pl.BlockSpec((tm, GLA_VW), lambda i, j: (i, 0)),
                  pl.BlockSpec((tm, GLA_VW), lambda i, j: (i, rb_blk)),
                  pl.BlockSpec((1, GLA_DV), lambda i, j: (0, 0)),
                  pl.BlockSpec((tm, tn), lambda i, j: (i, j)),
                  pl.BlockSpec((tm, tn), lambda i, j: (i, nb + j)),
                  pl.BlockSpec((None, SB_WIDTH, tn), lambda i, j: (layer, 0, j)),
                  pl.BlockSpec((None, GLA_VW, tn), lambda i, j: (layer, 0, j))],
        out_specs=pl.BlockSpec((tm, tn), lambda i, j: (i, j)),
        scratch_shapes=[pltpu.VMEM((tm, GLA_VW), act)],
        compiler_params=_params("parallel", "arbitrary"),
    )(o_a, o_b, z_main, gla_norm_g.reshape(1, GLA_DV), z_gates, z_gates, w_a_up, w_b_up)


def _router_kernel(h_ref, g_ref, wr_ref, br_ref, xn_ref, route_ref):
    x = h_ref[...]
    t = x * lax.rsqrt(jnp.mean(x * x, axis=-1, keepdims=True) + EPS) * g_ref[...]
    xn_ref[...] = t.astype(xn_ref.dtype)
    logits = jnp.dot(t, wr_ref[...], precision=lax.Precision.HIGHEST, preferred_element_type=F32) + br_ref[...]
    lane = lax.broadcasted_iota(jnp.int32, logits.shape, 1)
    is_group = lane < N_GROUPS
    gl = jnp.where(is_group, logits, NEG_BIG)
    g_max = jnp.max(gl, axis=1, keepdims=True)
    g_idx = jnp.min(jnp.where(gl == g_max, lane, BLK), axis=1, keepdims=True)
    g_sum = jnp.sum(jnp.where(is_group, jnp.exp(gl - g_max), 0.0), axis=1, keepdims=True)
    g_w = 1.0 / g_sum
    lo = N_GROUPS + EXPERTS_PER_GROUP * g_idx
    in_grp = jnp.logical_and(lane >= lo, lane < lo + EXPERTS_PER_GROUP)
    e1 = jnp.where(in_grp, logits, NEG_BIG)
    v1 = jnp.max(e1, axis=1, keepdims=True)
    i1 = jnp.min(jnp.where(e1 == v1, lane, BLK), axis=1, keepdims=True)
    e2 = jnp.where(lane == i1, NEG_BIG, e1)
    v2 = jnp.max(e2, axis=1, keepdims=True)
    i2 = jnp.min(jnp.where(jnp.logical_and(e2 == v2, in_grp), lane, BLK), axis=1, keepdims=True)
    r = jnp.exp(v2 - v1)
    w1 = g_w / (1.0 + r)
    w2 = w1 * r
    route = jnp.where(lane == 0, (i1 - N_GROUPS).astype(F32),
                      jnp.where(lane == 1, (i2 - N_GROUPS).astype(F32),
                                jnp.where(lane == 2, w1, jnp.where(lane == 3, w2, 0.0))))
    route_ref[...] = route


def router(h, norm_g, w_r, b_r):
    m, d = h.shape
    tm = _pick(m, (256, 128, 64, 16))
    return pl.pallas_call(
        _router_kernel,
        out_shape=(jax.ShapeDtypeStruct((m, d), F32), jax.ShapeDtypeStruct((m, BLK), F32)),
        grid=(m // tm,),
        in_specs=[pl.BlockSpec((tm, d), lambda i: (i, 0)),
                  pl.BlockSpec((1, d), lambda i: (0, 0)),
                  pl.BlockSpec((d, BLK), lambda i: (0, 0)),
                  pl.BlockSpec((1, BLK), lambda i: (0, 0))],
        out_specs=(pl.BlockSpec((tm, d), lambda i: (i, 0)), pl.BlockSpec((tm, BLK), lambda i: (i, 0))),
        compiler_params=_params("parallel"),
    )(h, norm_g.reshape(1, d), w_r, b_r)


def _expert_kernel(te_ref, nu_ref, slot_ref, x_hbm, sw_ref, wg_ref, wu_ref, wd_ref, o_ref, xbuf, sem):
    i = pl.program_id(0)
    n_used = nu_ref[0]
    cur = i % 2

    def gather_row(tile, buf, r):
        token = slot_ref[tile * EXPERT_TILE + r] >> 1
        return pltpu.make_async_copy(x_hbm.at[pl.ds(token, 1), :], xbuf.at[buf, pl.ds(r, 1), :], sem.at[buf])

    def start_gather(tile, buf):
        def body(r, carry):
            gather_row(tile, buf, r).start()
            return carry
        lax.fori_loop(0, EXPERT_TILE, body, 0)

    @pl.when(i == 0)
    def _():
        start_gather(0, 0)

    @pl.when(i + 1 < n_used)
    def _():
        start_gather(i + 1, 1 - cur)

    @pl.when(i < n_used)
    def _():
        def body(r, carry):
            gather_row(i, cur, r).wait()
            return carry
        lax.fori_loop(0, EXPERT_TILE, body, 0)
        x = xbuf[cur].astype(BF16)
        gate = jnp.dot(x, wg_ref[...].astype(BF16), preferred_element_type=F32)
        up = jnp.dot(x, wu_ref[...].astype(BF16), preferred_element_type=F32)
        hid = (gate * _sigmoid(gate)) * up
        hid = (hid * sw_ref[...]).astype(BF16)
        o_ref[...] = jnp.dot(hid, wd_ref[...].astype(BF16), preferred_element_type=F32)

    @pl.when(i >= n_used)
    def _():
        o_ref[...] = jnp.zeros_like(o_ref)


def expert_mlp(x, row_slot, slot_w, tile_expert, n_used, w_g, w_u, w_d, layer):
    d = x.shape[1]
    r = row_slot.shape[0]
    nt = r // EXPERT_TILE

    def w_map(i, te, nu, sl):
        return (layer, te[i], 0, 0)

    return pl.pallas_call(
        _expert_kernel,
        out_shape=jax.ShapeDtypeStruct((r, d), F32),
        grid_spec=pltpu.PrefetchScalarGridSpec(
            num_scalar_prefetch=3,
            grid=(nt,),
            in_specs=[pl.BlockSpec(memory_space=pl.ANY),
                      pl.BlockSpec((EXPERT_TILE, 1), lambda i, te, nu, sl: (jnp.minimum(i, nu[0] - 1), 0)),
                      pl.BlockSpec((None, None, d, D_EXPERT), w_map),
                      pl.BlockSpec((None, None, d, D_EXPERT), w_map),
                      pl.BlockSpec((None, None, D_EXPERT, d), w_map)],
            out_specs=pl.BlockSpec((EXPERT_TILE, d), lambda i, te, nu, sl: (i, 0)),
            scratch_shapes=[pltpu.VMEM((2, EXPERT_TILE, d), F32), pltpu.SemaphoreType.DMA((2,))],
        ),
        compiler_params=_params("arbitrary"),
    )(tile_expert, n_used, row_slot, x, slot_w, w_g, w_u, w_d)


def moe_dispatch(route, m):
    e_flat = route[:, 0:2].astype(jnp.int32).reshape(-1)
    w_flat = route[:, 2:4].reshape(-1)
    n_slots = 2 * m
    n_rows = n_slots + N_EXPERTS * EXPERT_TILE
    onehot = (e_flat[:, None] == jnp.arange(N_EXPERTS, dtype=jnp.int32)[None, :]).astype(jnp.int32)
    running = jnp.cumsum(onehot, axis=0)
    counts = running[-1]
    rank = jnp.sum(onehot * running, axis=1) - 1
    padded = ((counts + EXPERT_TILE - 1) // EXPERT_TILE) * EXPERT_TILE
    ends_p = jnp.cumsum(padded)
    starts_p = ends_p - padded
    pos = jnp.sum(onehot * starts_p[None, :], axis=1) + rank
    row_slot = jnp.zeros((n_rows,), jnp.int32).at[pos].set(jnp.arange(n_slots, dtype=jnp.int32))
    slot_w = jnp.take(w_flat, row_slot).reshape(n_rows, 1)
    n_used = (ends_p[-1] // EXPERT_TILE).astype(jnp.int32)
    tile_start = jnp.arange(n_rows // EXPERT_TILE, dtype=jnp.int32) * EXPERT_TILE
    tile_expert = jnp.sum((jnp.minimum(tile_start, ends_p[-1] - 1)[:, None] >= ends_p[None, :]).astype(jnp.int32),
                          axis=1)
    tile_expert = jnp.minimum(tile_expert, N_EXPERTS - 1).astype(jnp.int32)
    return row_slot, slot_w, pos.reshape(m, 2), tile_expert, n_used.reshape(1)


def kernel(x_prompt, x_sample, cache_k, cache_v, state_gla, page_table, meta, norm1_g, w_in, sb_bias, gla_w_a2, gla_b_a, gla_norm_g, w_a_up, w_b_up, w_out, norm2_g, w_router_group, b_router_group, w_router_expert, b_router_expert, w_exp_gate, w_exp_up, w_exp_down, final_norm_g):
    n_b, seq, d = x_prompt.shape
    n_s, n_t, _ = x_sample.shape
    depth = w_in.shape[0]
    assert seq % BLK == 0 and n_t <= SAMPLE_ROWS
    rows_per_seq = seq + BLK
    n_chunks = rows_per_seq // BLK
    m_prompt = n_b * rows_per_seq
    m = m_prompt + n_s * SAMPLE_ROWS

    meta_rows = jnp.broadcast_to(meta.astype(F32)[None], (n_b, N_META, d))
    h_p = jnp.concatenate([jnp.zeros((n_b, FRONT_PAD, d), F32), meta_rows, x_prompt], axis=1)
    h_s = jnp.pad(x_sample, ((0, 0), (0, SAMPLE_ROWS - n_t), (0, 0)))
    h = jnp.concatenate([h_p.reshape(m_prompt, d), h_s.reshape(n_s * SAMPLE_ROWS, d)], axis=0)

    zero_state = jnp.zeros((n_b, GLA_HEADS, GLA_DK, GLA_DV), F32)
    kp_rows, vp_rows, ks_rows, vs_rows, sp_list, ss_list = [], [], [], [], [], []
    m_sample = n_s * SAMPLE_ROWS
    sample_levels = tuple(s for s in GLA_LEVELS if s < pl.next_power_of_2(n_t))
    for l in range(depth):
        xn_p = rms_norm_rows(h, norm1_g[l], BF16, 0, m_prompt)
        xn_s = rms_norm_rows(h, norm1_g[l], F32, m_prompt, m_sample)
        w_gates = w_in[l][:, COL_GATES:]
        w_ab = jnp.pad(w_in[l][:, COL_AB:COL_AB + GLA_RANK], ((0, 0), (0, BLK - GLA_RANK)))
        w_a2 = jnp.pad(gla_w_a2[l], ((0, BLK - GLA_RANK), (0, 0)))
        z_main = matmul(xn_p, w_in, N_MAIN, layer=l)
        z_gates = matmul(xn_p, w_gates, 2 * d)
        lg = forget_gate(xn_p, w_ab, w_a2, gla_b_a[l])
        zs_main = matmul(xn_s, w_in, N_MAIN, layer=l, precise=True)
        zs_gates = matmul(xn_s, w_gates, 2 * d, precise=True)
        lgs = forget_gate(xn_s, w_ab, w_a2, gla_b_a[l], precise=True)

        z_s = zs_main.reshape(n_s, SAMPLE_ROWS, N_MAIN)[:, :n_t]
        ks_new = z_s[:, :, COL_KA:COL_KA + SB_WIDTH]
        vs_new = z_s[:, :, COL_VA:COL_VA + SB_WIDTH]

        o_a_p = sb_prompt(z_main, sb_bias[l], n_b, rows_per_seq, m_prompt)
        q_s = z_s[:, :, COL_QA:COL_QA + SB_WIDTH].reshape(n_s, n_t, SB_HEADS, SB_HEAD_DIM)
        o_a_s = sb_paged(q_s, ks_new, vs_new, cache_k, cache_v, page_table, sb_bias[l], l)

        o_b_p, s_p = gla(z_main, z_main, z_main, lg, COL_QB, COL_KB, COL_VB, zero_state, n_b, n_chunks, m_prompt)
        pad_s = lambda a: jnp.pad(a, ((0, 0), (0, BLK - n_t), (0, 0))).reshape(n_s * BLK, a.shape[-1])
        lg_s = lgs.reshape(n_s, SAMPLE_ROWS, GLA_KW)[:, :n_t]
        o_b_s, s_s = gla(pad_s(z_s[:, :, COL_QB:COL_KB]), pad_s(z_s[:, :, COL_KB:COL_VB]),
                         pad_s(z_s[:, :, COL_VB:COL_RB]), pad_s(lg_s), 0, 0, 0,
                         state_gla[l], n_s, 1, n_s * BLK, levels=sample_levels, precise=True)
        o_b_s = o_b_s.reshape(n_s, BLK, GLA_VW)[:, :SAMPLE_ROWS].reshape(m_sample, GLA_VW)

        mix_p = merge_mix(o_a_p, o_b_p, z_main, z_gates, gla_norm_g[l], w_a_up, w_b_up, l, d)
        mix_s = merge_mix(o_a_s, o_b_s, zs_main, zs_gates, gla_norm_g[l], w_a_up, w_b_up, l, d, precise=True)
        h = matmul(mix_p, w_out, d, residual=h, layer=l)
        h = matmul(mix_s, w_out, d, residual=h, row0=m_prompt, layer=l, precise=True)

        w_r = jnp.pad(jnp.concatenate([w_router_group[l], w_router_expert[l]], axis=1),
                      ((0, 0), (0, BLK - N_GROUPS - N_EXPERTS)))
        b_r = jnp.pad(jnp.concatenate([b_router_group[l], b_router_expert[l]]),
                      (0, BLK - N_GROUPS - N_EXPERTS)).reshape(1, BLK)
        xn2, route = router(h, norm2_g[l], w_r, b_r)
        row_slot, slot_w, pos, tile_expert, n_used = moe_dispatch(route, m)
        y_slots = expert_mlp(xn2, row_slot, slot_w, tile_expert, n_used, w_exp_gate, w_exp_up, w_exp_down, l)
        h = h + jnp.take(y_slots, pos[:, 0], axis=0) + jnp.take(y_slots, pos[:, 1], axis=0)

        kv_p = z_main[:m_prompt].reshape(n_b, rows_per_seq, N_MAIN)[:, FRONT_PAD:]
        kp_rows.append(kv_p[:, :, COL_KA:COL_KA + SB_WIDTH].reshape(n_b, -1, SB_HEADS, SB_HEAD_DIM))
        vp_rows.append(kv_p[:, :, COL_VA:COL_VA + SB_WIDTH].reshape(n_b, -1, SB_HEADS, SB_HEAD_DIM))
        ks_rows.append(ks_new.reshape(n_s, n_t, SB_HEADS, SB_HEAD_DIM))
        vs_rows.append(vs_new.reshape(n_s, n_t, SB_HEADS, SB_HEAD_DIM))
        sp_list.append(s_p)
        ss_list.append(s_s)

    yn = rms_norm_rows(h, final_norm_g, F32)
    y_prompt = yn[:m_prompt].reshape(n_b, rows_per_seq, d)[:, BLK:]
    y_sample = yn[m_prompt:].reshape(n_s, SAMPLE_ROWS, d)[:, :n_t]
    return (y_prompt, y_sample, jnp.stack(kp_rows), jnp.stack(vp_rows), jnp.stack(ks_rows),
            jnp.stack(vs_rows), jnp.stack(sp_list), jnp.stack(ss_list))
```

```python
import functools

import numpy as np
import jax
import jax.numpy as jnp
from jax import lax
from jax.experimental import pallas as pl
from jax.experimental.pallas import tpu as pltpu

F32 = jnp.float32
BF16 = jnp.bfloat16

N_META = 16
SB_HEADS = 8
SB_HEAD_DIM = 128
SB_WIDTH = SB_HEADS * SB_HEAD_DIM
GLA_HEADS = 4
GLA_DK = 128
GLA_DV = 256
GLA_KW = GLA_HEADS * GLA_DK
GLA_VW = GLA_HEADS * GLA_DV
GLA_RANK = 16
GLA_TAU = 16.0
N_GROUPS = 4
EXPERTS_PER_GROUP = 8
N_EXPERTS = N_GROUPS * EXPERTS_PER_GROUP
D_EXPERT = 256
EPS = 1e-6

BLK = 128
FRONT_PAD = BLK - N_META
SAMPLE_ROWS = 8
COL_QA, COL_KA, COL_VA = 0, SB_WIDTH, 2 * SB_WIDTH
COL_QB = 3 * SB_WIDTH
COL_KB = COL_QB + GLA_KW
COL_VB = COL_KB + GLA_KW
COL_RB = COL_VB + GLA_VW
N_MAIN = COL_RB + GLA_VW
COL_AB = N_MAIN
COL_GATES = N_MAIN + GLA_RANK

EXPERT_TILE = 256
VMEM_LIMIT = 48 * 1024 * 1024
NEG_BIG = -1e30


def _params(*sem):
    return pltpu.CompilerParams(dimension_semantics=sem, vmem_limit_bytes=VMEM_LIMIT)


def _pick(m, prefs):
    for p in prefs:
        if m % p == 0:
            return p
    return m


def _softplus(z):
    return jnp.maximum(z, 0.0) + jnp.log1p(jnp.exp(-jnp.abs(z)))


def _sigmoid(z):
    return 1.0 / (1.0 + jnp.exp(-z))


def _split_bf16(x):
    hi = x.astype(BF16)
    lo = (x - hi.astype(F32)).astype(BF16)
    return hi, lo


def _rms_kernel(x_ref, g_ref, o_ref):
    x = x_ref[...]
    ms = jnp.mean(x * x, axis=-1, keepdims=True)
    o_ref[...] = (x * lax.rsqrt(ms + EPS) * g_ref[...]).astype(o_ref.dtype)


def rms_norm_rows(x, g, out_dtype, row0=0, n_rows=None):
    d = x.shape[1]
    m = x.shape[0] - row0 if n_rows is None else n_rows
    tm = _pick(np.gcd(m, row0) if row0 else m, (256, 128, 64, 16))
    off = row0 // tm
    return pl.pallas_call(
        _rms_kernel,
        out_shape=jax.ShapeDtypeStruct((m, d), out_dtype),
        grid=(m // tm,),
        in_specs=[pl.BlockSpec((tm, d), lambda i: (i + off, 0)), pl.BlockSpec((1, d), lambda i: (0, 0))],
        out_specs=pl.BlockSpec((tm, d), lambda i: (i, 0)),
        compiler_params=_params("parallel"),
    )(x, g.reshape(1, d))


def _dot(a, b, precise, dims=None):
    dt, prec = (F32, lax.Precision.HIGHEST) if precise else (BF16, None)
    a, b = a.astype(dt), b.astype(dt)
    if dims is None:
        return jnp.dot(a, b, precision=prec, preferred_element_type=F32)
    return lax.dot_general(a, b, dims, precision=prec, preferred_element_type=F32)


def _mm_kernel(precise, x_ref, w_ref, o_ref):
    o_ref[...] = _dot(x_ref[...], w_ref[...], precise).astype(o_ref.dtype)


def _mm_res_kernel(precise, x_ref, w_ref, r_ref, o_ref):
    o_ref[...] = r_ref[...] + _dot(x_ref[...], w_ref[...], precise)


def matmul(x, w, n_out, out_dtype=F32, residual=None, row0=0, layer=None, precise=False):
    m, k = x.shape
    tm = _pick(int(np.gcd(m, row0)) if row0 else m, (2176, 1792, 1088, 1024, 896, 512, 256, 128))
    tn = _pick(n_out, (512, 256, 128))
    off = row0 // tm
    if layer is None:
        w_spec = pl.BlockSpec((k, tn), lambda i, j: (0, j))
    else:
        w_spec = pl.BlockSpec((None, k, tn), lambda i, j: (layer, 0, j))
    in_specs = [pl.BlockSpec((tm, k), lambda i, j: (i, 0)), w_spec]
    args = [x, w]
    body = _mm_kernel
    out_shape = jax.ShapeDtypeStruct((m, n_out), out_dtype)
    aliases = {}
    if residual is not None:
        in_specs.append(pl.BlockSpec((tm, tn), lambda i, j: (i + off, j)))
        args.append(residual)
        body = _mm_res_kernel
        out_shape = jax.ShapeDtypeStruct(residual.shape, residual.dtype)
        aliases = {2: 0}
    return pl.pallas_call(
        functools.partial(body, precise),
        out_shape=out_shape,
        grid=(m // tm, n_out // tn),
        in_specs=in_specs,
        out_specs=pl.BlockSpec((tm, tn), lambda i, j: (i + off, j)),
        input_output_aliases=aliases,
        compiler_params=_params("parallel", "parallel"),
    )(*args)


def _lg_kernel(precise, x_ref, wab_ref, wa2_ref, ba_ref, o_ref):
    ab = _dot(x_ref[...], wab_ref[...], precise)
    z = _dot(ab, wa2_ref[...], precise) + ba_ref[...]
    o_ref[...] = -_softplus(-z) * (1.0 / GLA_TAU)


def forget_gate(xn, w_ab, w_a2, b_a, precise=False):
    m, d = xn.shape
    tm = _pick(m, (1088, 896, 512, 256, 128))
    return pl.pallas_call(
        functools.partial(_lg_kernel, precise),
        out_shape=jax.ShapeDtypeStruct((m, GLA_KW), F32),
        grid=(m // tm,),
        in_specs=[pl.BlockSpec((tm, d), lambda i: (i, 0)),
                  pl.BlockSpec((d, BLK), lambda i: (0, 0)),
                  pl.BlockSpec((BLK, GLA_KW), lambda i: (0, 0)),
                  pl.BlockSpec((1, GLA_KW), lambda i: (0, 0))],
        out_specs=pl.BlockSpec((tm, GLA_KW), lambda i: (i, 0)),
        compiler_params=_params("parallel"),
    )(xn, w_ab, w_a2, b_a.reshape(1, GLA_KW))


SB_HEAD_GROUP = 4


def _later_and_total_matrix(width):
    row = lax.broadcasted_iota(jnp.int32, (width, width + BLK), 0)
    col = lax.broadcasted_iota(jnp.int32, (width, width + BLK), 1)
    return jnp.where(jnp.logical_or(row > col, col >= width), 1.0, 0.0).astype(BF16)


def _sb_prompt_kernel(bias_ref, q_ref, k_ref, v_ref, o_ref, acc_ref, run_ref):
    hg = pl.program_id(1)
    qi = pl.program_id(2)
    row = lax.broadcasted_iota(jnp.int32, (BLK, BLK), 0)
    col = lax.broadcasted_iota(jnp.int32, (BLK, BLK), 1)
    q_pos = qi * BLK + row
    scale = SB_HEAD_DIM ** -0.5
    acc_ref[...] = jnp.zeros_like(acc_ref)
    run_ref[...] = jnp.zeros_like(run_ref)

    def block(kb, n_blk, masked):
        width = n_blk * BLK
        off = pl.multiple_of(kb * BLK, BLK)
        sum_mat = _later_and_total_matrix(width)
        if masked:
            k_pos = kb * BLK + col
            mask = jnp.logical_and(k_pos < q_pos, k_pos >= FRONT_PAD)
        heads = range(SB_HEAD_GROUP)
        sls = [slice(h * SB_HEAD_DIM, (h + 1) * SB_HEAD_DIM) for h in heads]
        qs = [q_ref[:, sl].astype(BF16) for sl in sls]
        ks = [k_ref[pl.ds(off, width), sl].astype(BF16) for sl in sls]
        vs = [v_ref[pl.ds(off, width), sl].astype(BF16) for sl in sls]
        runs = [run_ref[h] for h in heads]
        accs = [acc_ref[h] for h in heads]
        nt = (((1,), (1,)), ((), ()))
        zs = [lax.dot_general(qs[h], ks[h], nt, preferred_element_type=F32) * scale
              + bias_ref[hg * SB_HEAD_GROUP + h] for h in heads]
        sps = [jnp.maximum(z, 0.0) + jnp.log(1.0 + jnp.exp(-jnp.abs(z))) for z in zs]
        stays = [jnp.where(mask, -sp, 0.0) if masked else -sp for sp in sps]
        splits = [_split_bf16(s) for s in stays]
        sums = [jnp.dot(hi, sum_mat, preferred_element_type=F32) + jnp.dot(lo, sum_mat, preferred_element_type=F32)
                for hi, lo in splits]
        run_w = [jnp.concatenate([r] * n_blk, axis=1) for r in runs]
        ws = [jnp.exp(zs[h] - sps[h] + run_w[h] + sums[h][:, :width]) for h in heads]
        if masked:
            ws = [jnp.where(mask, w, 0.0) for w in ws]
        pvs = [jnp.dot(ws[h].astype(BF16), vs[h], preferred_element_type=F32) for h in heads]
        for h in heads:
            acc_ref[h] = accs[h] + pvs[h]
            run_ref[h] = runs[h] + sums[h][:, width:]

    block(qi, 1, True)

    n_inner = jnp.maximum(qi - 1, 0)

    def body(i, carry):
        block(qi - 2 - 2 * i, 2, False)
        return carry

    lax.fori_loop(0, n_inner >> 1, body, 0)

    @pl.when((n_inner & 1) == 1)
    def _():
        block(1, 1, False)

    @pl.when(qi > 0)
    def _():
        block(0, 1, True)

    for h in range(SB_HEAD_GROUP):
        o_ref[:, h * SB_HEAD_DIM:(h + 1) * SB_HEAD_DIM] = acc_ref[h].astype(o_ref.dtype)


def sb_prompt(z_main, sb_bias, n_batch, rows_per_seq, m_total):
    nq = rows_per_seq // BLK
    gw = SB_HEAD_GROUP * SB_HEAD_DIM
    n_groups = SB_HEADS // SB_HEAD_GROUP
    kcol = COL_KA // gw
    vcol = COL_VA // gw
    return pl.pallas_call(
        _sb_prompt_kernel,
        out_shape=jax.ShapeDtypeStruct((m_total, SB_WIDTH), BF16),
        grid=(n_batch, n_groups, nq),
        in_specs=[pl.BlockSpec(memory_space=pltpu.SMEM),
                  pl.BlockSpec((BLK, gw), lambda b, g, i: (b * nq + i, g)),
                  pl.BlockSpec((rows_per_seq, gw), lambda b, g, i: (b, kcol + g)),
                  pl.BlockSpec((rows_per_seq, gw), lambda b, g, i: (b, vcol + g))],
        out_specs=pl.BlockSpec((BLK, gw), lambda b, g, i: (b * nq + i, g)),
        scratch_shapes=[pltpu.VMEM((SB_HEAD_GROUP, BLK, SB_HEAD_DIM), F32),
                        pltpu.VMEM((SB_HEAD_GROUP, BLK, BLK), F32)],
        compiler_params=_params("parallel", "parallel", "arbitrary"),
    )(sb_bias, z_main, z_main, z_main)


Q_ROWS = 16


PAGE_GROUP = 2


def _later_matrix(n_pages):
    idx = np.arange(n_pages * BLK)
    pj, ps = idx[:, None] // BLK, idx[None, :] // BLK
    later = (pj < ps) | ((pj == ps) & (idx[:, None] > idx[None, :]))
    return later.astype(np.float32)


def _sb_paged_kernel(pages_per_step, n_t, pt_ref, q_ref, bias_ref, lat1_ref, latg_ref, knew_ref, vnew_ref, *rest):
    k_refs = rest[:pages_per_step]
    v_refs = rest[pages_per_step:2 * pages_per_step]
    o_ref, acc_ref, run_ref = rest[2 * pages_per_step:]
    p = pl.program_id(1)
    n_rows = SB_HEADS * Q_ROWS
    scale = SB_HEAD_DIM ** -0.5
    nt = (((1,), (1,)), ((), ()))
    heads = range(SB_HEADS)

    def head_rows(refs, h):
        return jnp.concatenate([r[pl.ds(h, BLK, stride=SB_HEADS), :] for r in refs], axis=0).astype(BF16)

    def process(groups, later_ref, mask, run, accs):
        gs = range(len(groups))
        ks = [[head_rows(kl, h) for h in heads] for kl, _ in groups]
        vs = [[head_rows(vl, h) for h in heads] for _, vl in groups]
        bias = jnp.concatenate([bias_ref[...]] * len(groups[0][0]), axis=1)
        zs = [jnp.concatenate([lax.dot_general(q_ref[h], ks[g][h], nt, preferred_element_type=F32) for h in heads],
                              axis=0) * scale + bias for g in gs]
        sps = [jnp.maximum(z, 0.0) + jnp.log(1.0 + jnp.exp(-jnp.abs(z))) for z in zs]
        stays = [-sp if mask is None else jnp.where(mask, -sp, 0.0) for sp in sps]
        later = later_ref[...]
        sums = []
        for s in stays:
            hi, lo = _split_bf16(s)
            both = jnp.dot(jnp.concatenate([hi, lo], axis=0), later, preferred_element_type=F32)
            sums.append(both[:n_rows] + both[n_rows:])
        totals = [jnp.sum(s, axis=1, keepdims=True) for s in stays]
        ws = []
        for g in gs:
            w = jnp.exp(zs[g] - sps[g] + run + sums[g])
            ws.append((w if mask is None else jnp.where(mask, w, 0.0)).astype(BF16))
            run = run + totals[g]
        for g in gs:
            accs = [accs[h] + jnp.dot(ws[g][h * Q_ROWS:(h + 1) * Q_ROWS], vs[g][h], preferred_element_type=F32)
                    for h in heads]
        return run, accs

    @pl.when(p == 0)
    def _():
        t = lax.broadcasted_iota(jnp.int32, (n_rows, BLK), 0) & (Q_ROWS - 1)
        key = lax.broadcasted_iota(jnp.int32, (n_rows, BLK), 1)
        zero_acc = [jnp.zeros((Q_ROWS, SB_HEAD_DIM), F32) for _ in heads]
        run, accs = process([([knew_ref], [vnew_ref])], lat1_ref, key < t,
                            jnp.zeros((n_rows, 1), F32), zero_acc)
        run_ref[...] = run
        for h in heads:
            acc_ref[h] = accs[h]

    run = run_ref[...]
    accs = [acc_ref[h] for h in heads]
    groups = [(k_refs[g:g + PAGE_GROUP], v_refs[g:g + PAGE_GROUP]) for g in range(0, pages_per_step, PAGE_GROUP)]
    run, accs = process(groups, latg_ref, None, run, accs)
    run_ref[...] = run
    for h in heads:
        acc_ref[h] = accs[h]

    @pl.when(p == pl.num_programs(1) - 1)
    def _():
        valid = lax.broadcasted_iota(jnp.int32, (SAMPLE_ROWS, SB_HEAD_DIM), 0) < n_t
        for h in heads:
            o_ref[:, h * SB_HEAD_DIM:(h + 1) * SB_HEAD_DIM] = jnp.where(valid, accs[h][:SAMPLE_ROWS], 0.0)


def sb_paged(q_s, k_new, v_new, cache_k, cache_v, page_table, sb_bias, layer):
    n_seq, n_t = q_s.shape[0], q_s.shape[1]
    n_pages = page_table.shape[1]
    page = cache_k.shape[2]
    assert page == BLK and n_t <= SAMPLE_ROWS <= Q_ROWS and n_pages % PAGE_GROUP == 0
    pps = max(c for c in (4 * PAGE_GROUP, 2 * PAGE_GROUP, PAGE_GROUP) if n_pages % c == 0)
    later_1 = jnp.asarray(_later_matrix(1), dtype=BF16)
    later_g = jnp.asarray(_later_matrix(PAGE_GROUP), dtype=BF16)
    n_rows = SB_HEADS * Q_ROWS
    page_rows = page * SB_HEADS
    q = jnp.pad(jnp.transpose(q_s, (0, 2, 1, 3)), ((0, 0), (0, 0), (0, Q_ROWS - n_t), (0, 0))).astype(BF16)
    bias = jnp.broadcast_to(jnp.repeat(sb_bias.astype(F32), Q_ROWS)[:, None], (n_rows, BLK))
    pad = ((0, 0), (0, BLK - n_t), (0, 0))
    k_new = jnp.pad(k_new, pad).reshape(n_seq, page_rows, SB_HEAD_DIM)
    v_new = jnp.pad(v_new, pad).reshape(n_seq, page_rows, SB_HEAD_DIM)
    ck = cache_k.reshape(cache_k.shape[0], cache_k.shape[1], page_rows, SB_HEAD_DIM)
    cv = cache_v.reshape(cache_v.shape[0], cache_v.shape[1], page_rows, SB_HEAD_DIM)

    def page_spec(j):
        return pl.BlockSpec((None, None, page_rows, SB_HEAD_DIM),
                            lambda s, p, pt: (layer, pt[s, n_pages - 1 - (p * pps + j)], 0, 0))

    in_specs = ([pl.BlockSpec((None, SB_HEADS, Q_ROWS, SB_HEAD_DIM), lambda s, p, pt: (s, 0, 0, 0)),
                 pl.BlockSpec((n_rows, BLK), lambda s, p, pt: (0, 0)),
                 pl.BlockSpec(later_1.shape, lambda s, p, pt: (0, 0)),
                 pl.BlockSpec(later_g.shape, lambda s, p, pt: (0, 0)),
                 pl.BlockSpec((None, page_rows, SB_HEAD_DIM), lambda s, p, pt: (s, 0, 0)),
                 pl.BlockSpec((None, page_rows, SB_HEAD_DIM), lambda s, p, pt: (s, 0, 0))]
                + [page_spec(j) for j in range(pps)] + [page_spec(j) for j in range(pps)])
    out = pl.pallas_call(
        functools.partial(_sb_paged_kernel, pps, n_t),
        out_shape=jax.ShapeDtypeStruct((n_seq, SAMPLE_ROWS, SB_WIDTH), F32),
        grid_spec=pltpu.PrefetchScalarGridSpec(
            num_scalar_prefetch=1,
            grid=(n_seq, n_pages // pps),
            in_specs=in_specs,
            out_specs=pl.BlockSpec((None, SAMPLE_ROWS, SB_WIDTH), lambda s, p, pt: (s, 0, 0)),
            scratch_shapes=[pltpu.VMEM((SB_HEADS, Q_ROWS, SB_HEAD_DIM), F32), pltpu.VMEM((n_rows, 1), F32)],
        ),
        compiler_params=_params("parallel", "arbitrary"),
    )(page_table, q, bias, later_1, later_g, k_new, v_new, *([ck] * pps), *([cv] * pps))
    return out.reshape(n_seq * SAMPLE_ROWS, SB_WIDTH).astype(BF16)


GLA_LEVELS = (64, 32, 16, 8, 4, 2, 1)


def _gla_static_matrix(levels):
    idx = np.arange(BLK)
    mats = [(idx[None, :] <= idx[:, None]), (idx[None, :] > idx[:, None]), np.ones((BLK, BLK), bool)]
    for size in levels:
        blk = idx // size
        odd = (blk % 2) == 1
        ref = np.where(odd, blk * size - 1, (blk + 1) * size - 1)
        j = idx[None, :]
        q_side = (j > ref[:, None]) & (j <= idx[:, None])
        k_side = (j > idx[:, None]) & (j <= ref[:, None])
        mats.append(np.where(odd[:, None], q_side, k_side))
    return np.concatenate(mats, axis=0).astype(np.float32)


def _gla_kernel(levels, precise, n_valid, sm_ref, q_ref, k_ref, v_ref, g_ref, s0_ref, o_ref, sout_ref, state_ref):
    c = pl.program_id(2)
    rows_in = q_ref.shape[0]

    @pl.when(c == 0)
    def _():
        state_ref[...] = s0_ref[...]

    def load(ref):
        x = ref[...]
        if rows_in < BLK:
            x = jnp.concatenate([x, jnp.zeros((BLK - rows_in, x.shape[1]), F32)], axis=0)
        return x

    g = load(g_ref)
    if n_valid < BLK:
        g = jnp.where(lax.broadcasted_iota(jnp.int32, g.shape, 0) < n_valid, g, 0.0)
    g_hi, g_lo = _split_bf16(g)
    sm = sm_ref[...]
    expo = (jnp.dot(sm, g_hi, preferred_element_type=F32) + jnp.dot(sm, g_lo, preferred_element_type=F32))
    q = load(q_ref) * (GLA_DK ** -0.5)
    k = load(k_ref)
    v = load(v_ref)
    s_prev = state_ref[...]
    b = expo[0:BLK]
    rem = expo[BLK:2 * BLK]
    total = expo[2 * BLK:3 * BLK]

    row = lax.broadcasted_iota(jnp.int32, (BLK, BLK), 0)
    col = lax.broadcasted_iota(jnp.int32, (BLK, BLK), 1)
    nt = (((1,), (1,)), ((), ()))
    scores = jnp.where(row == col, jnp.sum(q * k, axis=1, keepdims=True), 0.0)
    for li, size in enumerate(levels):
        f = jnp.exp(expo[(3 + li) * BLK:(4 + li) * BLK])
        part = _dot(q * f, k * f, precise, nt)
        shift = size.bit_length() - 1
        rb = row >> shift
        pair = jnp.logical_and((rb & 1) == 1, (col >> shift) == rb - 1)
        scores = jnp.where(pair, part, scores)

    o_ref[...] = (_dot(q * jnp.exp(b), s_prev, precise) + _dot(scores, v, precise))[:rows_in]
    kv = _dot(jnp.transpose(k * jnp.exp(rem)), v, precise)
    decay = jnp.exp(jnp.transpose(total)[:, 0:1])
    s_new = decay * s_prev + kv
    state_ref[...] = s_new
    sout_ref[...] = s_new


def gla(q_arr, k_arr, v_arr, g_arr, q_col, k_col, v_col, s0, n_batch, n_chunks, m_out,
        levels=GLA_LEVELS, precise=False, rows=BLK, n_valid=BLK):
    sm = jnp.asarray(_gla_static_matrix(levels), dtype=BF16)
    qc, kc, vc = q_col // GLA_DK, k_col // GLA_DK, v_col // GLA_DV
    return pl.pallas_call(
        functools.partial(_gla_kernel, levels, precise, n_valid),
        out_shape=(jax.ShapeDtypeStruct((m_out, GLA_VW), F32),
                   jax.ShapeDtypeStruct((n_batch, GLA_HEADS, GLA_DK, GLA_DV), F32)),
        grid=(n_batch, GLA_HEADS, n_chunks),
        in_specs=[pl.BlockSpec(sm.shape, lambda b, h, c: (0, 0)),
                  pl.BlockSpec((rows, GLA_DK), lambda b, h, c: (b * n_chunks + c, qc + h)),
                  pl.BlockSpec((rows, GLA_DK), lambda b, h, c: (b * n_chunks + c, kc + h)),
                  pl.BlockSpec((rows, GLA_DV), lambda b, h, c: (b * n_chunks + c, vc + h)),
                  pl.BlockSpec((rows, GLA_DK), lambda b, h, c: (b * n_chunks + c, h)),
                  pl.BlockSpec((None, None, GLA_DK, GLA_DV), lambda b, h, c: (b, h, 0, 0))],
        out_specs=(pl.BlockSpec((rows, GLA_DV), lambda b, h, c: (b * n_chunks + c, h)),
                   pl.BlockSpec((None, None, GLA_DK, GLA_DV), lambda b, h, c: (b, h, 0, 0))),
        scratch_shapes=[pltpu.VMEM((GLA_DK, GLA_DV), F32)],
        compiler_params=_params("parallel", "parallel", "arbitrary"),
    )(sm, q_arr, k_arr, v_arr, g_arr, s0)


def _merge_kernel(precise, oa_ref, ob_ref, rb_ref, gn_ref, ga_ref, gb_ref, wa_ref, wb_ref, o_ref, obn_ref):
    @pl.when(pl.program_id(1) == 0)
    def _():
        for h in range(GLA_HEADS):
            sl = slice(h * GLA_DV, (h + 1) * GLA_DV)
            x = ob_ref[:, sl]
            y = x * lax.rsqrt(jnp.mean(x * x, axis=-1, keepdims=True) + EPS) * gn_ref[...]
            r = rb_ref[:, sl]
            obn_ref[:, sl] = (y * (r * _sigmoid(r))).astype(obn_ref.dtype)

    a = _dot(oa_ref[...], wa_ref[...], precise)
    b = _dot(obn_ref[...], wb_ref[...], precise)
    o_ref[...] = (_sigmoid(ga_ref[...]) * a + _sigmoid(gb_ref[...]) * b).astype(o_ref.dtype)


def merge_mix(o_a, o_b, z_main, z_gates, gla_norm_g, w_a_up, w_b_up, layer, d_model, precise=False):
    m = o_a.shape[0]
    tm = _pick(m, (1088, 896, 512, 256, 128))
    tn = _pick(d_model, (512, 256, 128))
    nb = d_model // tn
    rb_blk = COL_RB // GLA_VW
    act = F32 if precise else BF16
    return pl.pallas_call(
        functools.partial(_merge_kernel, precise),
        out_shape=jax.ShapeDtypeStruct((m, d_model), act),
        grid=(m // tm, nb),
        in_specs=[pl.BlockSpec((tm, SB_WIDTH), lambda i, j: (i, 0)),
                  pl.BlockSpec((tm, GLA_VW), lambda i, j: (i, 0)),
                  pl.BlockSpec((tm, GLA_VW), lambda i, j: (i, rb_blk)),
                  pl.BlockSpec((1, GLA_DV), lambda i, j: (0, 0)),
                  pl.BlockSpec((tm, tn), lambda i, j: (i, j)),
                  pl.BlockSpec((tm, tn), lambda i, j: (i, nb + j)),
                  pl.BlockSpec((None, SB_WIDTH, tn), lambda i, j: (layer, 0, j)),
                  pl.BlockSpec((None, GLA_VW, tn), lambda i, j: (layer, 0, j))],
        out_specs=pl.BlockSpec((tm, tn), lambda i, j: (i, j)),
        scratch_shapes=[pltpu.VMEM((tm, GLA_VW), act)],
        compiler_params=_params("parallel", "arbitrary"),
    )(o_a, o_b, z_main, gla_norm_g.reshape(1, GLA_DV), z_gates, z_gates, w_a_up, w_b_up)


def _router_kernel(h_ref, g_ref, wr_ref, br_ref, xn_ref, route_ref):
    x = h_ref[...]
    t = x * lax.rsqrt(jnp.mean(x * x, axis=-1, keepdims=True) + EPS) * g_ref[...]
    tm, n_sub = x.shape[0], x.shape[1] // BLK
    for s in range(n_sub):
        xn_ref[pl.ds(s, tm, stride=n_sub), :] = t[:, s * BLK:(s + 1) * BLK]
    logits = jnp.dot(t, wr_ref[...], precision=lax.Precision.HIGHEST, preferred_element_type=F32) + br_ref[...]
    lane = lax.broadcasted_iota(jnp.int32, logits.shape, 1)
    is_group = lane < N_GROUPS
    gl = jnp.where(is_group, logits, NEG_BIG)
    g_max = jnp.max(gl, axis=1, keepdims=True)
    g_idx = jnp.min(jnp.where(gl == g_max, lane, BLK), axis=1, keepdims=True)
    g_sum = jnp.sum(jnp.where(is_group, jnp.exp(gl - g_max), 0.0), axis=1, keepdims=True)
    g_w = 1.0 / g_sum
    lo = N_GROUPS + EXPERTS_PER_GROUP * g_idx
    in_grp = jnp.logical_and(lane >= lo, lane < lo + EXPERTS_PER_GROUP)
    e1 = jnp.where(in_grp, logits, NEG_BIG)
    v1 = jnp.max(e1, axis=1, keepdims=True)
    i1 = jnp.min(jnp.where(e1 == v1, lane, BLK), axis=1, keepdims=True)
    e2 = jnp.where(lane == i1, NEG_BIG, e1)
    v2 = jnp.max(e2, axis=1, keepdims=True)
    i2 = jnp.min(jnp.where(jnp.logical_and(e2 == v2, in_grp), lane, BLK), axis=1, keepdims=True)
    r = jnp.exp(v2 - v1)
    w1 = g_w / (1.0 + r)
    w2 = w1 * r
    route = jnp.where(lane == 0, (i1 - N_GROUPS).astype(F32),
                      jnp.where(lane == 1, (i2 - N_GROUPS).astype(F32),
                                jnp.where(lane == 2, w1, jnp.where(lane == 3, w2, 0.0))))
    route_ref[...] = route


def router(h, norm_g, w_r, b_r):
    m, d = h.shape
    tm = _pick(m, (256, 128, 64, 16))
    n_sub = d // BLK
    return pl.pallas_call(
        _router_kernel,
        out_shape=(jax.ShapeDtypeStruct((m * n_sub, BLK), F32), jax.ShapeDtypeStruct((m, BLK), F32)),
        grid=(m // tm,),
        in_specs=[pl.BlockSpec((tm, d), lambda i: (i, 0)),
                  pl.BlockSpec((1, d), lambda i: (0, 0)),
                  pl.BlockSpec((d, BLK), lambda i: (0, 0)),
                  pl.BlockSpec((1, BLK), lambda i: (0, 0))],
        out_specs=(pl.BlockSpec((tm * n_sub, BLK), lambda i: (i, 0)), pl.BlockSpec((tm, BLK), lambda i: (i, 0))),
        compiler_params=_params("parallel"),
    )(h, norm_g.reshape(1, d), w_r, b_r)


def _expert_kernel(te_ref, nu_ref, slot_ref, x_hbm, sw_ref, wg_ref, wu_ref, wd_ref, o_ref, xbuf, sem):
    i = pl.program_id(0)
    n_used = nu_ref[0]
    cur = i % 2
    n_sub = x_hbm.shape[1]

    def gather_row(tile, buf, r):
        token = slot_ref[tile * EXPERT_TILE + r] >> 1
        dst = xbuf.at[buf, pl.ds(pl.multiple_of(r * n_sub, n_sub), n_sub), :]
        return pltpu.make_async_copy(x_hbm.at[token], dst, sem.at[buf])

    def start_gather(tile, buf):
        def body(r, carry):
            gather_row(tile, buf, r).start()
            return carry
        lax.fori_loop(0, EXPERT_TILE, body, 0)

    @pl.when(i == 0)
    def _():
        start_gather(0, 0)

    @pl.when(i + 1 < n_used)
    def _():
        start_gather(i + 1, 1 - cur)

    @pl.when(i < n_used)
    def _():
        def body(r, carry):
            gather_row(i, cur, r).wait()
            return carry
        lax.fori_loop(0, EXPERT_TILE, body, 0)
        x = jnp.concatenate([xbuf[cur, pl.ds(s, EXPERT_TILE, stride=n_sub), :] for s in range(n_sub)],
                            axis=1).astype(BF16)
        gate =jnp.dot(x, wg_ref[...].astype(BF16), preferred_element_type=F32)
        up = jnp.dot(x, wu_ref[...].astype(BF16), preferred_element_type=F32)
        hid = (gate * _sigmoid(gate)) * up
        hid = (hid * sw_ref[...]).astype(BF16)
        o_ref[...] = jnp.dot(hid, wd_ref[...].astype(BF16), preferred_element_type=F32)

    @pl.when(i >= n_used)
    def _():
        o_ref[...] = jnp.zeros_like(o_ref)


def expert_mlp(x, row_slot, slot_w, tile_expert, n_used, w_g, w_u, w_d, layer):
    n_sub = x.shape[1]
    d = n_sub * BLK
    r = row_slot.shape[0]
    nt = r // EXPERT_TILE

    def w_map(i, te, nu, sl):
        return (layer, te[i], 0, 0)

    return pl.pallas_call(
        _expert_kernel,
        out_shape=jax.ShapeDtypeStruct((r, d), F32),
        grid_spec=pltpu.PrefetchScalarGridSpec(
            num_scalar_prefetch=3,
            grid=(nt,),
            in_specs=[pl.BlockSpec(memory_space=pl.ANY),
                      pl.BlockSpec((EXPERT_TILE, 1), lambda i, te, nu, sl: (jnp.minimum(i, nu[0] - 1), 0)),
                      pl.BlockSpec((None, None, d, D_EXPERT), w_map),
                      pl.BlockSpec((None, None, d, D_EXPERT), w_map),
                      pl.BlockSpec((None, None, D_EXPERT, d), w_map)],
            out_specs=pl.BlockSpec((EXPERT_TILE, d), lambda i, te, nu, sl: (i, 0)),
            scratch_shapes=[pltpu.VMEM((2, EXPERT_TILE * n_sub, BLK), F32), pltpu.SemaphoreType.DMA((2,))],
        ),
        compiler_params=_params("arbitrary"),
    )(tile_expert, n_used, row_slot, x, slot_w, w_g, w_u, w_d)


def moe_dispatch(route, m):
    e_flat = route[:, 0:2].astype(jnp.int32).reshape(-1)
    w_flat = route[:, 2:4].reshape(-1)
    n_slots = 2 * m
    n_rows = n_slots + N_EXPERTS * EXPERT_TILE
    onehot = (e_flat[:, None] == jnp.arange(N_EXPERTS, dtype=jnp.int32)[None, :]).astype(jnp.int32)
    running = jnp.cumsum(onehot, axis=0)
    counts = running[-1]
    rank = jnp.sum(onehot * running, axis=1) - 1
    padded = ((counts + EXPERT_TILE - 1) // EXPERT_TILE) * EXPERT_TILE
    ends_p = jnp.cumsum(padded)
    starts_p = ends_p - padded
    pos = jnp.sum(onehot * starts_p[None, :], axis=1) + rank
    row_slot = jnp.zeros((n_rows,), jnp.int32).at[pos].set(jnp.arange(n_slots, dtype=jnp.int32))
    slot_w = jnp.take(w_flat, row_slot).reshape(n_rows, 1)
    n_used = (ends_p[-1] // EXPERT_TILE).astype(jnp.int32)
    tile_start = jnp.arange(n_rows // EXPERT_TILE, dtype=jnp.int32) * EXPERT_TILE
    tile_expert = jnp.sum((jnp.minimum(tile_start, ends_p[-1] - 1)[:, None] >= ends_p[None, :]).astype(jnp.int32),
                          axis=1)
    tile_expert = jnp.minimum(tile_expert, N_EXPERTS - 1).astype(jnp.int32)
    return row_slot, slot_w, pos.reshape(m, 2), tile_expert, n_used.reshape(1)


def kernel(x_prompt, x_sample, cache_k, cache_v, state_gla, page_table, meta, norm1_g, w_in, sb_bias, gla_w_a2, gla_b_a, gla_norm_g, w_a_up, w_b_up, w_out, norm2_g, w_router_group, b_router_group, w_router_expert, b_router_expert, w_exp_gate, w_exp_up, w_exp_down, final_norm_g):
    n_b, seq, d = x_prompt.shape
    n_s, n_t, _ = x_sample.shape
    depth = w_in.shape[0]
    assert seq % BLK == 0 and n_t <= SAMPLE_ROWS
    rows_per_seq = seq + BLK
    n_chunks = rows_per_seq // BLK
    m_prompt = n_b * rows_per_seq
    m = m_prompt + n_s * SAMPLE_ROWS

    meta_rows = jnp.broadcast_to(meta.astype(F32)[None], (n_b, N_META, d))
    h_p = jnp.concatenate([jnp.zeros((n_b, FRONT_PAD, d), F32), meta_rows, x_prompt], axis=1)
    h_s = jnp.pad(x_sample, ((0, 0), (0, SAMPLE_ROWS - n_t), (0, 0)))
    h = jnp.concatenate([h_p.reshape(m_prompt, d), h_s.reshape(n_s * SAMPLE_ROWS, d)], axis=0)

    zero_state = jnp.zeros((n_b, GLA_HEADS, GLA_DK, GLA_DV), F32)
    kp_rows, vp_rows, ks_rows, vs_rows, sp_list, ss_list = [], [], [], [], [], []
    m_sample = n_s * SAMPLE_ROWS
    sample_levels = tuple(s for s in GLA_LEVELS if s < pl.next_power_of_2(n_t))
    for l in range(depth):
        xn_p = rms_norm_rows(h, norm1_g[l], BF16, 0, m_prompt)
        xn_s = rms_norm_rows(h, norm1_g[l], F32, m_prompt, m_sample)
        w_gates = w_in[l][:, COL_GATES:]
        w_ab = jnp.pad(w_in[l][:, COL_AB:COL_AB + GLA_RANK], ((0, 0), (0, BLK - GLA_RANK)))
        w_a2 = jnp.pad(gla_w_a2[l], ((0, BLK - GLA_RANK), (0, 0)))
        z_main = matmul(xn_p, w_in, N_MAIN, layer=l)
        z_gates = matmul(xn_p, w_gates, 2 * d)
        lg = forget_gate(xn_p, w_ab, w_a2, gla_b_a[l])
        zs_main = matmul(xn_s, w_in, N_MAIN, layer=l, precise=True)
        zs_gates = matmul(xn_s, w_gates, 2 * d, precise=True)
        lgs = forget_gate(xn_s, w_ab, w_a2, gla_b_a[l], precise=True)

        z_s = zs_main.reshape(n_s, SAMPLE_ROWS, N_MAIN)[:, :n_t]
        ks_new = z_s[:, :, COL_KA:COL_KA + SB_WIDTH]
        vs_new = z_s[:, :, COL_VA:COL_VA + SB_WIDTH]

        o_a_p = sb_prompt(z_main, sb_bias[l], n_b, rows_per_seq, m_prompt)
        q_s = z_s[:, :, COL_QA:COL_QA + SB_WIDTH].reshape(n_s, n_t, SB_HEADS, SB_HEAD_DIM)
        o_a_s = sb_paged(q_s, ks_new, vs_new, cache_k, cache_v, page_table, sb_bias[l], l)

        o_b_p, s_p = gla(z_main, z_main, z_main, lg, COL_QB, COL_KB, COL_VB, zero_state, n_b, n_chunks, m_prompt)
        o_b_s, s_s = gla(zs_main, zs_main, zs_main, lgs, COL_QB, COL_KB, COL_VB, state_gla[l], n_s, 1, m_sample,
                         levels=sample_levels, precise=True, rows=SAMPLE_ROWS, n_valid=n_t)

        mix_p = merge_mix(o_a_p, o_b_p, z_main, z_gates, gla_norm_g[l], w_a_up, w_b_up, l, d)
        mix_s = merge_mix(o_a_s, o_b_s, zs_main, zs_gates, gla_norm_g[l], w_a_up, w_b_up, l, d, precise=True)
        h = matmul(mix_p, w_out, d, residual=h, layer=l)
        h = matmul(mix_s, w_out, d, residual=h, row0=m_prompt, layer=l, precise=True)

        w_r = jnp.pad(jnp.concatenate([w_router_group[l], w_router_expert[l]], axis=1),
                      ((0, 0), (0, BLK - N_GROUPS - N_EXPERTS)))
        b_r = jnp.pad(jnp.concatenate([b_router_group[l], b_router_expert[l]]),
                      (0, BLK - N_GROUPS - N_EXPERTS)).reshape(1, BLK)
        xn2, route = router(h, norm2_g[l], w_r, b_r)
        row_slot, slot_w, pos, tile_expert, n_used = moe_dispatch(route, m)
        y_slots = expert_mlp(xn2.reshape(m, d // BLK, BLK), row_slot, slot_w, tile_expert, n_used,
                             w_exp_gate, w_exp_up, w_exp_down, l)
        h = h + jnp.take(y_slots, pos[:, 0], axis=0) + jnp.take(y_slots, pos[:, 1], axis=0)

        kv_p = z_main[:m_prompt].reshape(n_b, rows_per_seq, N_MAIN)[:, FRONT_PAD:]
        kp_rows.append(kv_p[:, :, COL_KA:COL_KA + SB_WIDTH].reshape(n_b, -1, SB_HEADS, SB_HEAD_DIM))
        vp_rows.append(kv_p[:, :, COL_VA:COL_VA + SB_WIDTH].reshape(n_b, -1, SB_HEADS, SB_HEAD_DIM))
        ks_rows.append(ks_new.reshape(n_s, n_t, SB_HEADS, SB_HEAD_DIM))
        vs_rows.append(vs_new.reshape(n_s, n_t, SB_HEADS, SB_HEAD_DIM))
        sp_list.append(s_p)
        ss_list.append(s_s)

    yn = rms_norm_rows(h, final_norm_g, F32)
    y_prompt = yn[:m_prompt].reshape(n_b, rows_per_seq, d)[:, BLK:]
    y_sample = yn[m_prompt:].reshape(n_s, SAMPLE_ROWS, d)[:, :n_t]
    return (y_prompt, y_sample, jnp.stack(kp_rows), jnp.stack(vp_rows), jnp.stack(ks_rows),
            jnp.stack(vs_rows), jnp.stack(sp_list), jnp.stack(ss_list))
```

```python
import functools

import numpy as np
import jax
import jax.numpy as jnp
from jax import lax
from jax.experimental import pallas as pl
from jax.experimental.pallas import tpu as pltpu

F32 = jnp.float32
BF16 = jnp.bfloat16

N_META = 16
SB_HEADS = 8
SB_HEAD_DIM = 128
SB_WIDTH = SB_HEADS * SB_HEAD_DIM
GLA_HEADS = 4
GLA_DK = 128
GLA_DV = 256
GLA_KW = GLA_HEADS * GLA_DK
GLA_VW = GLA_HEADS * GLA_DV
GLA_RANK = 16
GLA_TAU = 16.0
N_GROUPS = 4
EXPERTS_PER_GROUP = 8
N_EXPERTS = N_GROUPS * EXPERTS_PER_GROUP
D_EXPERT = 256
EPS = 1e-6

BLK = 128
FRONT_PAD = BLK - N_META
SAMPLE_ROWS = 8
COL_QA, COL_KA, COL_VA = 0, SB_WIDTH, 2 * SB_WIDTH
COL_QB = 3 * SB_WIDTH
COL_KB = COL_QB + GLA_KW
COL_VB = COL_KB + GLA_KW
COL_RB = COL_VB + GLA_VW
N_MAIN = COL_RB + GLA_VW
COL_AB = N_MAIN
COL_GATES = N_MAIN + GLA_RANK

EXPERT_TILE = 256
VMEM_LIMIT = 48 * 1024 * 1024
NEG_BIG = -1e30


def _params(*sem):
    return pltpu.CompilerParams(dimension_semantics=sem, vmem_limit_bytes=VMEM_LIMIT)


def _pick(m, prefs):
    for p in prefs:
        if m % p == 0:
            return p
    return m


def _softplus(z):
    return jnp.maximum(z, 0.0) + jnp.log1p(jnp.exp(-jnp.abs(z)))


def _sigmoid(z):
    return 1.0 / (1.0 + jnp.exp(-z))


def _split_bf16(x):
    hi = x.astype(BF16)
    lo = (x - hi.astype(F32)).astype(BF16)
    return hi, lo


def _rms_kernel(x_ref, g_ref, o_ref):
    x = x_ref[...]
    ms = jnp.mean(x * x, axis=-1, keepdims=True)
    o_ref[...] = (x * lax.rsqrt(ms + EPS) * g_ref[...]).astype(o_ref.dtype)


def rms_norm_rows(x, g, out_dtype, row0=0, n_rows=None):
    d = x.shape[1]
    m = x.shape[0] - row0 if n_rows is None else n_rows
    tm = _pick(np.gcd(m, row0) if row0 else m, (256, 128, 64, 16))
    off = row0 // tm
    return pl.pallas_call(
        _rms_kernel,
        out_shape=jax.ShapeDtypeStruct((m, d), out_dtype),
        grid=(m // tm,),
        in_specs=[pl.BlockSpec((tm, d), lambda i: (i + off, 0)), pl.BlockSpec((1, d), lambda i: (0, 0))],
        out_specs=pl.BlockSpec((tm, d), lambda i: (i, 0)),
        compiler_params=_params("parallel"),
    )(x, g.reshape(1, d))


def _dot(a, b, precise, dims=None):
    dt, prec = (F32, lax.Precision.HIGHEST) if precise else (BF16, None)
    a, b = a.astype(dt), b.astype(dt)
    if dims is None:
        return jnp.dot(a, b, precision=prec, preferred_element_type=F32)
    return lax.dot_general(a, b, dims, precision=prec, preferred_element_type=F32)


def _mm_kernel(precise, x_ref, w_ref, o_ref):
    o_ref[...] = _dot(x_ref[...], w_ref[...], precise).astype(o_ref.dtype)


def _mm_res_kernel(precise, x_ref, w_ref, r_ref, o_ref):
    o_ref[...] = r_ref[...] + _dot(x_ref[...], w_ref[...], precise)


def matmul(x, w, n_out, out_dtype=F32, residual=None, row0=0, layer=None, precise=False):
    m, k = x.shape
    tm = _pick(int(np.gcd(m, row0)) if row0 else m, (2176, 1792, 1088, 1024, 896, 512, 256, 128))
    tn = _pick(n_out, (512, 256, 128))
    off = row0 // tm
    if layer is None:
        w_spec = pl.BlockSpec((k, tn), lambda i, j: (0, j))
    else:
        w_spec = pl.BlockSpec((None, k, tn), lambda i, j: (layer, 0, j))
    in_specs = [pl.BlockSpec((tm, k), lambda i, j: (i, 0)), w_spec]
    args = [x, w]
    body = _mm_kernel
    out_shape = jax.ShapeDtypeStruct((m, n_out), out_dtype)
    aliases = {}
    if residual is not None:
        in_specs.append(pl.BlockSpec((tm, tn), lambda i, j: (i + off, j)))
        args.append(residual)
        body = _mm_res_kernel
        out_shape = jax.ShapeDtypeStruct(residual.shape, residual.dtype)
        aliases = {2: 0}
    return pl.pallas_call(
        functools.partial(body, precise),
        out_shape=out_shape,
        grid=(m // tm, n_out // tn),
        in_specs=in_specs,
        out_specs=pl.BlockSpec((tm, tn), lambda i, j: (i + off, j)),
        input_output_aliases=aliases,
        compiler_params=_params("parallel", "parallel"),
    )(*args)


def _lg_kernel(precise, x_ref, wab_ref, wa2_ref, ba_ref, o_ref):
    ab = _dot(x_ref[...], wab_ref[...], precise)
    z = _dot(ab, wa2_ref[...], precise) + ba_ref[...]
    o_ref[...] = -_softplus(-z) * (1.0 / GLA_TAU)


def forget_gate(xn, w_ab, w_a2, b_a, precise=False):
    m, d = xn.shape
    tm = _pick(m, (1088, 896, 512, 256, 128))
    return pl.pallas_call(
        functools.partial(_lg_kernel, precise),
        out_shape=jax.ShapeDtypeStruct((m, GLA_KW), F32),
        grid=(m // tm,),
        in_specs=[pl.BlockSpec((tm, d), lambda i: (i, 0)),
                  pl.BlockSpec((d, BLK), lambda i: (0, 0)),
                  pl.BlockSpec((BLK, GLA_KW), lambda i: (0, 0)),
                  pl.BlockSpec((1, GLA_KW), lambda i: (0, 0))],
        out_specs=pl.BlockSpec((tm, GLA_KW), lambda i: (i, 0)),
        compiler_params=_params("parallel"),
    )(xn, w_ab, w_a2, b_a.reshape(1, GLA_KW))


SB_HEAD_GROUP = 4


def _later_and_total_matrix(width):
    row = lax.broadcasted_iota(jnp.int32, (width, width + BLK), 0)
    col = lax.broadcasted_iota(jnp.int32, (width, width + BLK), 1)
    return jnp.where(jnp.logical_or(row > col, col >= width), 1.0, 0.0).astype(BF16)


def _sb_prompt_kernel(bias_ref, q_ref, k_ref, v_ref, o_ref, acc_ref, run_ref):
    hg = pl.program_id(1)
    qi = pl.program_id(2)
    row = lax.broadcasted_iota(jnp.int32, (BLK, BLK), 0)
    col = lax.broadcasted_iota(jnp.int32, (BLK, BLK), 1)
    q_pos = qi * BLK + row
    scale = SB_HEAD_DIM ** -0.5
    acc_ref[...] = jnp.zeros_like(acc_ref)
    run_ref[...] = jnp.zeros_like(run_ref)

    def block(kb, n_blk, masked):
        width = n_blk * BLK
        off = pl.multiple_of(kb * BLK, BLK)
        sum_mat = _later_and_total_matrix(width)
        if masked:
            k_pos = kb * BLK + col
            mask = jnp.logical_and(k_pos < q_pos, k_pos >= FRONT_PAD)
        heads = range(SB_HEAD_GROUP)
        sls = [slice(h * SB_HEAD_DIM, (h + 1) * SB_HEAD_DIM) for h in heads]
        qs = [q_ref[:, sl].astype(BF16) for sl in sls]
        ks = [k_ref[pl.ds(off, width), sl].astype(BF16) for sl in sls]
        vs = [v_ref[pl.ds(off, width), sl].astype(BF16) for sl in sls]
        runs = [run_ref[h] for h in heads]
        accs = [acc_ref[h] for h in heads]
        nt = (((1,), (1,)), ((), ()))
        zs = [lax.dot_general(qs[h], ks[h], nt, preferred_element_type=F32) * scale
              + bias_ref[hg * SB_HEAD_GROUP + h] for h in heads]
        sps = [jnp.maximum(z, 0.0) + jnp.log(1.0 + jnp.exp(-jnp.abs(z))) for z in zs]
        stays = [jnp.where(mask, -sp, 0.0) if masked else -sp for sp in sps]
        splits = [_split_bf16(s) for s in stays]
        sums = [jnp.dot(hi, sum_mat, preferred_element_type=F32) + jnp.dot(lo, sum_mat, preferred_element_type=F32)
                for hi, lo in splits]
        run_w = [jnp.concatenate([r] * n_blk, axis=1) for r in runs]
        ws = [jnp.exp(zs[h] - sps[h] + run_w[h] + sums[h][:, :width]) for h in heads]
        if masked:
            ws = [jnp.where(mask, w, 0.0) for w in ws]
        pvs = [jnp.dot(ws[h].astype(BF16), vs[h], preferred_element_type=F32) for h in heads]
        for h in heads:
            acc_ref[h] = accs[h] + pvs[h]
            run_ref[h] = runs[h] + sums[h][:, width:]

    block(qi, 1, True)

    n_inner = jnp.maximum(qi - 1, 0)

    def body(i, carry):
        block(qi - 2 - 2 * i, 2, False)
        return carry

    lax.fori_loop(0, n_inner >> 1, body, 0)

    @pl.when((n_inner & 1) == 1)
    def _():
        block(1, 1, False)

    @pl.when(qi > 0)
    def _():
        block(0, 1, True)

    for h in range(SB_HEAD_GROUP):
        o_ref[:, h * SB_HEAD_DIM:(h + 1) * SB_HEAD_DIM] = acc_ref[h].astype(o_ref.dtype)


def sb_prompt(z_main, sb_bias, n_batch, rows_per_seq, m_total):
    nq = rows_per_seq // BLK
    gw = SB_HEAD_GROUP * SB_HEAD_DIM
    n_groups = SB_HEADS // SB_HEAD_GROUP
    kcol = COL_KA // gw
    vcol = COL_VA // gw
    return pl.pallas_call(
        _sb_prompt_kernel,
        out_shape=jax.ShapeDtypeStruct((m_total, SB_WIDTH), BF16),
        grid=(n_batch, n_groups, nq),
        in_specs=[pl.BlockSpec(memory_space=pltpu.SMEM),
                  pl.BlockSpec((BLK, gw), lambda b, g, i: (b * nq + i, g)),
                  pl.BlockSpec((rows_per_seq, gw), lambda b, g, i: (b, kcol + g)),
                  pl.BlockSpec((rows_per_seq, gw), lambda b, g, i: (b, vcol + g))],
        out_specs=pl.BlockSpec((BLK, gw), lambda b, g, i: (b * nq + i, g)),
        scratch_shapes=[pltpu.VMEM((SB_HEAD_GROUP, BLK, SB_HEAD_DIM), F32),
                        pltpu.VMEM((SB_HEAD_GROUP, BLK, BLK), F32)],
        compiler_params=_params("parallel", "parallel", "arbitrary"),
    )(sb_bias, z_main, z_main, z_main)


Q_ROWS = 16


PAGE_GROUP = 2


def _later_matrix(n_pages):
    idx = np.arange(n_pages * BLK)
    pj, ps = idx[:, None] // BLK, idx[None, :] // BLK
    later = (pj < ps) | ((pj == ps) & (idx[:, None] > idx[None, :]))
    return later.astype(np.float32)


def _sb_paged_kernel(pages_per_step, n_t, pt_ref, q_ref, bias_ref, lat1_ref, latg_ref, knew_ref, vnew_ref, *rest):
    k_refs = rest[:pages_per_step]
    v_refs = rest[pages_per_step:2 * pages_per_step]
    o_ref, acc_ref, run_ref = rest[2 * pages_per_step:]
    p = pl.program_id(1)
    n_rows = SB_HEADS * Q_ROWS
    scale = SB_HEAD_DIM ** -0.5
    nt = (((1,), (1,)), ((), ()))
    heads = range(SB_HEADS)

    def head_rows(refs, h):
        parts = []
        for r in refs:
            n_keys = r.shape[0] // SB_HEADS
            parts.append(r[pl.ds(h, n_keys, stride=SB_HEADS), :])
            if n_keys < BLK:
                parts.append(jnp.zeros((BLK - n_keys, SB_HEAD_DIM), F32))
        return jnp.concatenate(parts, axis=0).astype(BF16)

    def process(groups, later_ref, mask, run, accs):
        gs = range(len(groups))
        ks = [[head_rows(kl, h) for h in heads] for kl, _ in groups]
        vs = [[head_rows(vl, h) for h in heads] for _, vl in groups]
        bias = jnp.concatenate([bias_ref[...]] * len(groups[0][0]), axis=1)
        zs = [jnp.concatenate([lax.dot_general(q_ref[h], ks[g][h], nt, preferred_element_type=F32) for h in heads],
                              axis=0) * scale + bias for g in gs]
        sps = [jnp.maximum(z, 0.0) + jnp.log(1.0 + jnp.exp(-jnp.abs(z))) for z in zs]
        stays = [-sp if mask is None else jnp.where(mask, -sp, 0.0) for sp in sps]
        later = later_ref[...]
        sums = []
        for s in stays:
            hi, lo = _split_bf16(s)
            both = jnp.dot(jnp.concatenate([hi, lo], axis=0), later, preferred_element_type=F32)
            sums.append(both[:n_rows] + both[n_rows:])
        totals = [jnp.sum(s, axis=1, keepdims=True) for s in stays]
        ws = []
        for g in gs:
            w = jnp.exp(zs[g] - sps[g] + run + sums[g])
            ws.append((w if mask is None else jnp.where(mask, w, 0.0)).astype(BF16))
            run = run + totals[g]
        for g in gs:
            accs = [accs[h] + jnp.dot(ws[g][h * Q_ROWS:(h + 1) * Q_ROWS], vs[g][h], preferred_element_type=F32)
                    for h in heads]
        return run, accs

    @pl.when(p == 0)
    def _():
        t = lax.broadcasted_iota(jnp.int32, (n_rows, BLK), 0) & (Q_ROWS - 1)
        key = lax.broadcasted_iota(jnp.int32, (n_rows, BLK), 1)
        zero_acc = [jnp.zeros((Q_ROWS, SB_HEAD_DIM), F32) for _ in heads]
        run, accs = process([([knew_ref], [vnew_ref])], lat1_ref, key < t,
                            jnp.zeros((n_rows, 1), F32), zero_acc)
        run_ref[...] = run
        for h in heads:
            acc_ref[h] = accs[h]

    run = run_ref[...]
    accs = [acc_ref[h] for h in heads]
    groups = [(k_refs[g:g + PAGE_GROUP], v_refs[g:g + PAGE_GROUP]) for g in range(0, pages_per_step, PAGE_GROUP)]
    run, accs = process(groups, latg_ref, None, run, accs)
    run_ref[...] = run
    for h in heads:
        acc_ref[h] = accs[h]

    @pl.when(p == pl.num_programs(1) - 1)
    def _():
        valid = lax.broadcasted_iota(jnp.int32, (SAMPLE_ROWS, SB_HEAD_DIM), 0) < n_t
        for h in heads:
            o_ref[:, h * SB_HEAD_DIM:(h + 1) * SB_HEAD_DIM] = jnp.where(valid, accs[h][:SAMPLE_ROWS], 0.0)


def sb_paged(q_s, k_new, v_new, cache_k, cache_v, page_table, sb_bias, layer):
    n_seq, n_t = q_s.shape[0], q_s.shape[1]
    n_pages = page_table.shape[1]
    page = cache_k.shape[2]
    assert page == BLK and n_t <= SAMPLE_ROWS <= Q_ROWS and n_pages % PAGE_GROUP == 0
    pps = max(c for c in (4 * PAGE_GROUP, 2 * PAGE_GROUP, PAGE_GROUP) if n_pages % c == 0)
    later_1 = jnp.asarray(_later_matrix(1), dtype=BF16)
    later_g = jnp.asarray(_later_matrix(PAGE_GROUP), dtype=BF16)
    n_rows = SB_HEADS * Q_ROWS
    page_rows = page * SB_HEADS
    q = jnp.pad(jnp.transpose(q_s, (0, 2, 1, 3)), ((0, 0), (0, 0), (0, Q_ROWS - n_t), (0, 0))).astype(BF16)
    bias = jnp.broadcast_to(jnp.repeat(sb_bias.astype(F32), Q_ROWS)[:, None], (n_rows, BLK))
    pad = ((0, 0), (0, SAMPLE_ROWS - n_t), (0, 0))
    new_rows = SAMPLE_ROWS * SB_HEADS
    k_new = jnp.pad(k_new, pad).reshape(n_seq, new_rows, SB_HEAD_DIM)
    v_new = jnp.pad(v_new, pad).reshape(n_seq, new_rows, SB_HEAD_DIM)
    ck = cache_k.reshape(cache_k.shape[0], cache_k.shape[1], page_rows, SB_HEAD_DIM)
    cv = cache_v.reshape(cache_v.shape[0], cache_v.shape[1], page_rows, SB_HEAD_DIM)

    def page_spec(j):
        return pl.BlockSpec((None, None, page_rows, SB_HEAD_DIM),
                            lambda s, p, pt: (layer, pt[s, n_pages - 1 - (p * pps + j)], 0, 0))

    in_specs = ([pl.BlockSpec((None, SB_HEADS, Q_ROWS, SB_HEAD_DIM), lambda s, p, pt: (s, 0, 0, 0)),
                 pl.BlockSpec((n_rows, BLK), lambda s, p, pt: (0, 0)),
                 pl.BlockSpec(later_1.shape, lambda s, p, pt: (0, 0)),
                 pl.BlockSpec(later_g.shape, lambda s, p, pt: (0, 0)),
                 pl.BlockSpec((None, new_rows, SB_HEAD_DIM), lambda s, p, pt: (s, 0, 0)),
                 pl.BlockSpec((None, new_rows, SB_HEAD_DIM), lambda s, p, pt: (s, 0, 0))]
                + [page_spec(j) for j in range(pps)] + [page_spec(j) for j in range(pps)])
    out = pl.pallas_call(
        functools.partial(_sb_paged_kernel, pps, n_t),
        out_shape=jax.ShapeDtypeStruct((n_seq, SAMPLE_ROWS, SB_WIDTH), F32),
        grid_spec=pltpu.PrefetchScalarGridSpec(
            num_scalar_prefetch=1,
            grid=(n_seq, n_pages // pps),
            in_specs=in_specs,
            out_specs=pl.BlockSpec((None, SAMPLE_ROWS, SB_WIDTH), lambda s, p, pt: (s, 0, 0)),
            scratch_shapes=[pltpu.VMEM((SB_HEADS, Q_ROWS, SB_HEAD_DIM), F32), pltpu.VMEM((n_rows, 1), F32)],
        ),
        compiler_params=_params("parallel", "arbitrary"),
    )(page_table, q, bias, later_1, later_g, k_new, v_new, *([ck] * pps), *([cv] * pps))
    return out.reshape(n_seq * SAMPLE_ROWS, SB_WIDTH).astype(BF16)


GLA_LEVELS = (64, 32, 16, 8, 4, 2, 1)


def _gla_static_matrix(levels):
    idx = np.arange(BLK)
    mats = [(idx[None, :] <= idx[:, None]), (idx[None, :] > idx[:, None]), np.ones((BLK, BLK), bool)]
    for size in levels:
        blk = idx // size
        odd = (blk % 2) == 1
        ref = np.where(odd, blk * size - 1, (blk + 1) * size - 1)
        j = idx[None, :]
        q_side = (j > ref[:, None]) & (j <= idx[:, None])
        k_side = (j > idx[:, None]) & (j <= ref[:, None])
        mats.append(np.where(odd[:, None], q_side, k_side))
    return np.concatenate(mats, axis=0).astype(np.float32)


def _gla_kernel(levels, precise, n_valid, sm_ref, q_ref, k_ref, v_ref, g_ref, s0_ref, o_ref, sout_ref, state_ref):
    c = pl.program_id(2)
    rows_in = q_ref.shape[0]

    @pl.when(c == 0)
    def _():
        state_ref[...] = s0_ref[...]

    def load(ref):
        x = ref[...]
        if rows_in < BLK:
            x = jnp.concatenate([x, jnp.zeros((BLK - rows_in, x.shape[1]), F32)], axis=0)
        return x

    g = load(g_ref)
    if n_valid < BLK:
        g = jnp.where(lax.broadcasted_iota(jnp.int32, g.shape, 0) < n_valid, g, 0.0)
    g_hi, g_lo = _split_bf16(g)
    sm = sm_ref[...]
    expo = (jnp.dot(sm, g_hi, preferred_element_type=F32) + jnp.dot(sm, g_lo, preferred_element_type=F32))
    q = load(q_ref) * (GLA_DK ** -0.5)
    k = load(k_ref)
    v = load(v_ref)
    s_prev = state_ref[...]
    b = expo[0:BLK]
    rem = expo[BLK:2 * BLK]
    total = expo[2 * BLK:3 * BLK]

    row = lax.broadcasted_iota(jnp.int32, (BLK, BLK), 0)
    col = lax.broadcasted_iota(jnp.int32, (BLK, BLK), 1)
    nt = (((1,), (1,)), ((), ()))
    scores = jnp.where(row == col, jnp.sum(q * k, axis=1, keepdims=True), 0.0)
    for li, size in enumerate(levels):
        f = jnp.exp(expo[(3 + li) * BLK:(4 + li) * BLK])
        part = _dot(q * f, k * f, precise, nt)
        shift = size.bit_length() - 1
        rb = row >> shift
        pair = jnp.logical_and((rb & 1) == 1, (col >> shift) == rb - 1)
        scores = jnp.where(pair, part, scores)

    o_ref[...] = (_dot(q * jnp.exp(b), s_prev, precise) + _dot(scores, v, precise))[:rows_in]
    kv = _dot(jnp.transpose(k * jnp.exp(rem)), v, precise)
    decay = jnp.exp(jnp.transpose(total)[:, 0:1])
    s_new = decay * s_prev + kv
    state_ref[...] = s_new
    sout_ref[...] = s_new


def gla(q_arr, k_arr, v_arr, g_arr, q_col, k_col, v_col, s0, n_batch, n_chunks, m_out,
        levels=GLA_LEVELS, precise=False, rows=BLK, n_valid=BLK):
    sm = jnp.asarray(_gla_static_matrix(levels), dtype=BF16)
    qc, kc, vc = q_col // GLA_DK, k_col // GLA_DK, v_col // GLA_DV
    return pl.pallas_call(
        functools.partial(_gla_kernel, levels, precise, n_valid),
        out_shape=(jax.ShapeDtypeStruct((m_out, GLA_VW), F32),
                   jax.ShapeDtypeStruct((n_batch, GLA_HEADS, GLA_DK, GLA_DV), F32)),
        grid=(n_batch, GLA_HEADS, n_chunks),
        in_specs=[pl.BlockSpec(sm.shape, lambda b, h, c: (0, 0)),
                  pl.BlockSpec((rows, GLA_DK), lambda b, h, c: (b * n_chunks + c, qc + h)),
                  pl.BlockSpec((rows, GLA_DK), lambda b, h, c: (b * n_chunks + c, kc + h)),
                  pl.BlockSpec((rows, GLA_DV), lambda b, h, c: (b * n_chunks + c, vc + h)),
                  pl.BlockSpec((rows, GLA_DK), lambda b, h, c: (b * n_chunks + c, h)),
                  pl.BlockSpec((None, None, GLA_DK, GLA_DV), lambda b, h, c: (b, h, 0, 0))],
        out_specs=(pl.BlockSpec((rows, GLA_DV), lambda b, h, c: (b * n_chunks + c, h)),
                   pl.BlockSpec((None, None, GLA_DK, GLA_DV), lambda b, h, c: (b, h, 0, 0))),
        scratch_shapes=[pltpu.VMEM((GLA_DK, GLA_DV), F32)],
        compiler_params=_params("parallel", "parallel", "arbitrary"),
    )(sm, q_arr, k_arr, v_arr, g_arr, s0)


def _merge_kernel(precise, oa_ref, ob_ref, rb_ref, gn_ref, ga_ref, gb_ref, wa_ref, wb_ref, o_ref, obn_ref):
    @pl.when(pl.program_id(1) == 0)
    def _():
        for h in range(GLA_HEADS):
            sl = slice(h * GLA_DV, (h + 1) * GLA_DV)
            x = ob_ref[:, sl]
            y = x * lax.rsqrt(jnp.mean(x * x, axis=-1, keepdims=True) + EPS) * gn_ref[...]
            r = rb_ref[:, sl]
            obn_ref[:, sl] = (y * (r * _sigmoid(r))).astype(obn_ref.dtype)

    a = _dot(oa_ref[...], wa_ref[...], precise)
    b = _dot(obn_ref[...], wb_ref[...], precise)
    o_ref[...] = (_sigmoid(ga_ref[...]) * a + _sigmoid(gb_ref[...]) * b).astype(o_ref.dtype)


def merge_mix(o_a, o_b, z_main, z_gates, gla_norm_g, w_a_up, w_b_up, layer, d_model, precise=False):
    m = o_a.shape[0]
    tm = _pick(m, (1088, 896, 512, 256, 128))
    tn = _pick(d_model, (512, 256, 128))
    nb = d_model // tn
    rb_blk = COL_RB // GLA_VW
    act = F32 if precise else BF16
    return pl.pallas_call(
        functools.partial(_merge_kernel, precise),
        out_shape=jax.ShapeDtypeStruct((m, d_model), act),
        grid=(m // tm, nb),
        in_specs=[pl.BlockSpec((tm, SB_WIDTH), lambda i, j: (i, 0)),
                  pl.BlockSpec((tm, GLA_VW), lambda i, j: (i, 0)),
                  pl.BlockSpec((tm, GLA_VW), lambda i, j: (i, rb_blk)),
                  pl.BlockSpec((1, GLA_DV), lambda i, j: (0, 0)),
                  pl.BlockSpec((tm, tn), lambda i, j: (i, j)),
                  pl.BlockSpec((tm, tn), lambda i, j: (i, nb + j)),
                  pl.BlockSpec((None, SB_WIDTH, tn), lambda i, j: (layer, 0, j)),
                  pl.BlockSpec((None, GLA_VW, tn), lambda i, j: (layer, 0, j))],
        out_specs=pl.BlockSpec((tm, tn), lambda i, j: (i, j)),
        scratch_shapes=[pltpu.VMEM((tm, GLA_VW), act)],
        compiler_params=_params("parallel", "arbitrary"),
    )(o_a, o_b, z_main, gla_norm_g.reshape(1, GLA_DV), z_gates, z_gates, w_a_up, w_b_up)


def _router_kernel(h_ref, g_ref, wr_ref, br_ref, xn_ref, route_ref):
    x = h_ref[...]
    t = x * lax.rsqrt(jnp.mean(x * x, axis=-1, keepdims=True) + EPS) * g_ref[...]
    tm, n_sub = x.shape[0], x.shape[1] // BLK
    for s in range(n_sub):
        xn_ref[pl.ds(s, tm, stride=n_sub), :] = t[:, s * BLK:(s + 1) * BLK]
    logits = jnp.dot(t, wr_ref[...], precision=lax.Precision.HIGHEST, preferred_element_type=F32) + br_ref[...]
    lane = lax.broadcasted_iota(jnp.int32, logits.shape, 1)
    is_group = lane < N_GROUPS
    gl = jnp.where(is_group, logits, NEG_BIG)
    g_max = jnp.max(gl, axis=1, keepdims=True)
    g_idx = jnp.min(jnp.where(gl == g_max, lane, BLK), axis=1, keepdims=True)
    g_sum = jnp.sum(jnp.where(is_group, jnp.exp(gl - g_max), 0.0), axis=1, keepdims=True)
    g_w = 1.0 / g_sum
    lo = N_GROUPS + EXPERTS_PER_GROUP * g_idx
    in_grp = jnp.logical_and(lane >= lo, lane < lo + EXPERTS_PER_GROUP)
    e1 = jnp.where(in_grp, logits, NEG_BIG)
    v1 = jnp.max(e1, axis=1, keepdims=True)
    i1 = jnp.min(jnp.where(e1 == v1, lane, BLK), axis=1, keepdims=True)
    e2 = jnp.where(lane == i1, NEG_BIG, e1)
    v2 = jnp.max(e2, axis=1, keepdims=True)
    i2 = jnp.min(jnp.where(jnp.logical_and(e2 == v2, in_grp), lane, BLK), axis=1, keepdims=True)
    r = jnp.exp(v2 - v1)
    w1 = g_w / (1.0 + r)
    w2 = w1 * r
    route = jnp.where(lane == 0, (i1 - N_GROUPS).astype(F32),
                      jnp.where(lane == 1, (i2 - N_GROUPS).astype(F32),
                                jnp.where(lane == 2, w1, jnp.where(lane == 3, w2, 0.0))))
    route_ref[...] = route


def router(h, norm_g, w_r, b_r):
    m, d = h.shape
    tm = _pick(m, (256, 128, 64, 16))
    n_sub = d // BLK
    return pl.pallas_call(
        _router_kernel,
        out_shape=(jax.ShapeDtypeStruct((m * n_sub, BLK), F32), jax.ShapeDtypeStruct((m, BLK), F32)),
        grid=(m // tm,),
        in_specs=[pl.BlockSpec((tm, d), lambda i: (i, 0)),
                  pl.BlockSpec((1, d), lambda i: (0, 0)),
                  pl.BlockSpec((d, BLK), lambda i: (0, 0)),
                  pl.BlockSpec((1, BLK), lambda i: (0, 0))],
        out_specs=(pl.BlockSpec((tm * n_sub, BLK), lambda i: (i, 0)), pl.BlockSpec((tm, BLK), lambda i: (i, 0))),
        compiler_params=_params("parallel"),
    )(h, norm_g.reshape(1, d), w_r, b_r)


def _expert_kernel(te_ref, nu_ref, slot_ref, x_hbm, sw_ref, wg_ref, wu_ref, wd_ref, o_ref, xbuf, sem):
    i = pl.program_id(0)
    n_used = nu_ref[0]
    cur = i % 2
    n_sub = x_hbm.shape[1]

    def gather_row(tile, buf, r):
        token = slot_ref[tile * EXPERT_TILE + r] >> 1
        dst = xbuf.at[buf, pl.ds(pl.multiple_of(r * n_sub, n_sub), n_sub), :]
        return pltpu.make_async_copy(x_hbm.at[token], dst, sem.at[buf])

    def start_gather(tile, buf):
        def body(r, carry):
            gather_row(tile, buf, r).start()
            return carry
        lax.fori_loop(0, EXPERT_TILE, body, 0)

    @pl.when(i == 0)
    def _():
        start_gather(0, 0)

    @pl.when(i + 1 < n_used)
    def _():
        start_gather(i + 1, 1 - cur)

    @pl.when(i < n_used)
    def _():
        pltpu.make_async_copy(xbuf.at[cur], xbuf.at[cur], sem.at[cur]).wait()
        x = jnp.concatenate([xbuf[cur, pl.ds(s, EXPERT_TILE, stride=n_sub), :] for s in range(n_sub)],
                            axis=1).astype(BF16)
        gate =jnp.dot(x, wg_ref[...].astype(BF16), preferred_element_type=F32)
        up = jnp.dot(x, wu_ref[...].astype(BF16), preferred_element_type=F32)
        hid = (gate * _sigmoid(gate)) * up
        hid = (hid * sw_ref[...]).astype(BF16)
        o_ref[...] = jnp.dot(hid, wd_ref[...].astype(BF16), preferred_element_type=F32)

    @pl.when(i >= n_used)
    def _():
        o_ref[...] = jnp.zeros_like(o_ref)


def expert_mlp(x, row_slot, slot_w, tile_expert, n_used, w_g, w_u, w_d, layer):
    n_sub = x.shape[1]
    d = n_sub * BLK
    r = row_slot.shape[0]
    nt = r // EXPERT_TILE

    def w_map(i, te, nu, sl):
        return (layer, te[i], 0, 0)

    return pl.pallas_call(
        _expert_kernel,
        out_shape=jax.ShapeDtypeStruct((r, d), F32),
        grid_spec=pltpu.PrefetchScalarGridSpec(
            num_scalar_prefetch=3,
            grid=(nt,),
            in_specs=[pl.BlockSpec(memory_space=pl.ANY),
                      pl.BlockSpec((EXPERT_TILE, 1), lambda i, te, nu, sl: (jnp.minimum(i, nu[0] - 1), 0)),
                      pl.BlockSpec((None, None, d, D_EXPERT), w_map),
                      pl.BlockSpec((None, None, d, D_EXPERT), w_map),
                      pl.BlockSpec((None, None, D_EXPERT, d), w_map)],
            out_specs=pl.BlockSpec((EXPERT_TILE, d), lambda i, te, nu, sl: (i, 0)),
            scratch_shapes=[pltpu.VMEM((2, EXPERT_TILE * n_sub, BLK), F32), pltpu.SemaphoreType.DMA((2,))],
        ),
        compiler_params=pltpu.CompilerParams(dimension_semantics=("arbitrary",), vmem_limit_bytes=VMEM_LIMIT,
                                             disable_bounds_checks=True),
    )(tile_expert, n_used, row_slot, x, slot_w, w_g, w_u, w_d)


def moe_dispatch(route, m):
    e_flat = route[:, 0:2].astype(jnp.int32).reshape(-1)
    w_flat = route[:, 2:4].reshape(-1)
    n_slots = 2 * m
    n_rows = n_slots + N_EXPERTS * EXPERT_TILE
    onehot = (e_flat[:, None] == jnp.arange(N_EXPERTS, dtype=jnp.int32)[None, :]).astype(jnp.int32)
    running = jnp.cumsum(onehot, axis=0)
    counts = running[-1]
    rank = jnp.sum(onehot * running, axis=1) - 1
    padded = ((counts + EXPERT_TILE - 1) // EXPERT_TILE) * EXPERT_TILE
    ends_p = jnp.cumsum(padded)
    starts_p = ends_p - padded
    pos = jnp.sum(onehot * starts_p[None, :], axis=1) + rank
    row_slot = jnp.zeros((n_rows,), jnp.int32).at[pos].set(jnp.arange(n_slots, dtype=jnp.int32))
    slot_w = jnp.take(w_flat, row_slot).reshape(n_rows, 1)
    n_used = (ends_p[-1] // EXPERT_TILE).astype(jnp.int32)
    tile_start = jnp.arange(n_rows // EXPERT_TILE, dtype=jnp.int32) * EXPERT_TILE
    tile_expert = jnp.sum((jnp.minimum(tile_start, ends_p[-1] - 1)[:, None] >= ends_p[None, :]).astype(jnp.int32),
                          axis=1)
    tile_expert = jnp.minimum(tile_expert, N_EXPERTS - 1).astype(jnp.int32)
    return row_slot, slot_w, pos.reshape(m, 2), tile_expert, n_used.reshape(1)


def kernel(x_prompt, x_sample, cache_k, cache_v, state_gla, page_table, meta, norm1_g, w_in, sb_bias, gla_w_a2, gla_b_a, gla_norm_g, w_a_up, w_b_up, w_out, norm2_g, w_router_group, b_router_group, w_router_expert, b_router_expert, w_exp_gate, w_exp_up, w_exp_down, final_norm_g):
    n_b, seq, d = x_prompt.shape
    n_s, n_t, _ = x_sample.shape
    depth = w_in.shape[0]
    assert seq % BLK == 0 and n_t <= SAMPLE_ROWS
    rows_per_seq = seq + BLK
    n_chunks = rows_per_seq // BLK
    m_prompt = n_b * rows_per_seq
    m = m_prompt + n_s * SAMPLE_ROWS

    meta_rows = jnp.broadcast_to(meta.astype(F32)[None], (n_b, N_META, d))
    h_p = jnp.concatenate([jnp.zeros((n_b, FRONT_PAD, d), F32), meta_rows, x_prompt], axis=1)
    h_s = jnp.pad(x_sample, ((0, 0), (0, SAMPLE_ROWS - n_t), (0, 0)))
    h = jnp.concatenate([h_p.reshape(m_prompt, d), h_s.reshape(n_s * SAMPLE_ROWS, d)], axis=0)

    zero_state = jnp.zeros((n_b, GLA_HEADS, GLA_DK, GLA_DV), F32)
    kp_rows, vp_rows, ks_rows, vs_rows, sp_list, ss_list = [], [], [], [], [], []
    m_sample = n_s * SAMPLE_ROWS
    sample_levels = tuple(s for s in GLA_LEVELS if s < pl.next_power_of_2(n_t))
    for l in range(depth):
        xn_p = rms_norm_rows(h, norm1_g[l], BF16, 0, m_prompt)
        xn_s = rms_norm_rows(h, norm1_g[l], F32, m_prompt, m_sample)
        w_gates = w_in[l][:, COL_GATES:]
        w_ab = jnp.pad(w_in[l][:, COL_AB:COL_AB + GLA_RANK], ((0, 0), (0, BLK - GLA_RANK)))
        w_a2 = jnp.pad(gla_w_a2[l], ((0, BLK - GLA_RANK), (0, 0)))
        z_main = matmul(xn_p, w_in, N_MAIN, layer=l)
        z_gates = matmul(xn_p, w_gates, 2 * d)
        lg = forget_gate(xn_p, w_ab, w_a2, gla_b_a[l])
        zs_main = matmul(xn_s, w_in, N_MAIN, layer=l, precise=True)
        zs_gates = matmul(xn_s, w_gates, 2 * d, precise=True)
        lgs = forget_gate(xn_s, w_ab, w_a2, gla_b_a[l], precise=True)

        z_s = zs_main.reshape(n_s, SAMPLE_ROWS, N_MAIN)[:, :n_t]
        ks_new = z_s[:, :, COL_KA:COL_KA + SB_WIDTH]
        vs_new = z_s[:, :, COL_VA:COL_VA + SB_WIDTH]

        o_a_p = sb_prompt(z_main, sb_bias[l], n_b, rows_per_seq, m_prompt)
        q_s = z_s[:, :, COL_QA:COL_QA + SB_WIDTH].reshape(n_s, n_t, SB_HEADS, SB_HEAD_DIM)
        o_a_s = sb_paged(q_s, ks_new, vs_new, cache_k, cache_v, page_table, sb_bias[l], l)

        o_b_p, s_p = gla(z_main, z_main, z_main, lg, COL_QB, COL_KB, COL_VB, zero_state, n_b, n_chunks, m_prompt)
        o_b_s, s_s = gla(zs_main, zs_main, zs_main, lgs, COL_QB, COL_KB, COL_VB, state_gla[l], n_s, 1, m_sample,
                         levels=sample_levels, precise=True, rows=SAMPLE_ROWS, n_valid=n_t)

        mix_p = merge_mix(o_a_p, o_b_p, z_main, z_gates, gla_norm_g[l], w_a_up, w_b_up, l, d)
        mix_s = merge_mix(o_a_s, o_b_s, zs_main, zs_gates, gla_norm_g[l], w_a_up, w_b_up, l, d, precise=True)
        h = matmul(mix_p, w_out, d, residual=h, layer=l)
        h = matmul(mix_s, w_out, d, residual=h, row0=m_prompt, layer=l, precise=True)

        w_r = jnp.pad(jnp.concatenate([w_router_group[l], w_router_expert[l]], axis=1),
                      ((0, 0), (0, BLK - N_GROUPS - N_EXPERTS)))
        b_r = jnp.pad(jnp.concatenate([b_router_group[l], b_router_expert[l]]),
                      (0, BLK - N_GROUPS - N_EXPERTS)).reshape(1, BLK)
        xn2, route = router(h, norm2_g[l], w_r, b_r)
        row_slot, slot_w, pos, tile_expert, n_used = moe_dispatch(route, m)
        y_slots = expert_mlp(xn2.reshape(m, d // BLK, BLK), row_slot, slot_w, tile_expert, n_used,
                             w_exp_gate, w_exp_up, w_exp_down, l)
        h = h + jnp.take(y_slots, pos[:, 0], axis=0) + jnp.take(y_slots, pos[:, 1], axis=0)

        kv_p = z_main[:m_prompt].reshape(n_b, rows_per_seq, N_MAIN)[:, FRONT_PAD:]
        kp_rows.append(kv_p[:, :, COL_KA:COL_KA + SB_WIDTH].reshape(n_b, -1, SB_HEADS, SB_HEAD_DIM))
        vp_rows.append(kv_p[:, :, COL_VA:COL_VA + SB_WIDTH].reshape(n_b, -1, SB_HEADS, SB_HEAD_DIM))
        ks_rows.append(ks_new.reshape(n_s, n_t, SB_HEADS, SB_HEAD_DIM))
        vs_rows.append(vs_new.reshape(n_s, n_t, SB_HEADS, SB_HEAD_DIM))
        sp_list.append(s_p)
        ss_list.append(s_s)

    yn = rms_norm_rows(h, final_norm_g, F32)
    y_prompt = yn[:m_prompt].reshape(n_b, rows_per_seq, d)[:, BLK:]
    y_sample = yn[m_prompt:].reshape(n_s, SAMPLE_ROWS, d)[:, :n_t]
    return (y_prompt, y_sample, jnp.stack(kp_rows), jnp.stack(vp_rows), jnp.stack(ks_rows),
            jnp.stack(vs_rows), jnp.stack(sp_list), jnp.stack(ss_list))
```

```python
import functools

import numpy as np
import jax
import jax.numpy as jnp
from jax import lax
from jax.experimental import pallas as pl
from jax.experimental.pallas import tpu as pltpu

F32 = jnp.float32
BF16 = jnp.bfloat16

N_META = 16
SB_HEADS = 8
SB_HEAD_DIM = 128
SB_WIDTH = SB_HEADS * SB_HEAD_DIM
GLA_HEADS = 4
GLA_DK = 128
GLA_DV = 256
GLA_KW = GLA_HEADS * GLA_DK
GLA_VW = GLA_HEADS * GLA_DV
GLA_RANK = 16
GLA_TAU = 16.0
N_GROUPS = 4
EXPERTS_PER_GROUP = 8
N_EXPERTS = N_GROUPS * EXPERTS_PER_GROUP
D_EXPERT = 256
EPS = 1e-6

BLK = 128
FRONT_PAD = BLK - N_META
SAMPLE_ROWS = 8
COL_QA, COL_KA, COL_VA = 0, SB_WIDTH, 2 * SB_WIDTH
COL_QB = 3 * SB_WIDTH
COL_KB = COL_QB + GLA_KW
COL_VB = COL_KB + GLA_KW
COL_RB = COL_VB + GLA_VW
N_MAIN = COL_RB + GLA_VW
COL_AB = N_MAIN
COL_GATES = N_MAIN + GLA_RANK

EXPERT_TILE = 256
VMEM_LIMIT = 48 * 1024 * 1024
NEG_BIG = -1e30


def _params(*sem):
    return pltpu.CompilerParams(dimension_semantics=sem, vmem_limit_bytes=VMEM_LIMIT)


def _pick(m, prefs):
    for p in prefs:
        if m % p == 0:
            return p
    return m


def _softplus(z):
    return jnp.maximum(z, 0.0) + jnp.log1p(jnp.exp(-jnp.abs(z)))


def _sigmoid(z):
    return 1.0 / (1.0 + jnp.exp(-z))


def _split_bf16(x):
    hi = x.astype(BF16)
    lo = (x - hi.astype(F32)).astype(BF16)
    return hi, lo


def _rms_kernel(x_ref, g_ref, o_ref):
    x = x_ref[...]
    ms = jnp.mean(x * x, axis=-1, keepdims=True)
    o_ref[...] = (x * lax.rsqrt(ms + EPS) * g_ref[...]).astype(o_ref.dtype)


def rms_norm_rows(x, g, out_dtype, row0=0, n_rows=None):
    d = x.shape[1]
    m = x.shape[0] - row0 if n_rows is None else n_rows
    tm = _pick(np.gcd(m, row0) if row0 else m, (256, 128, 64, 16))
    off = row0 // tm
    return pl.pallas_call(
        _rms_kernel,
        out_shape=jax.ShapeDtypeStruct((m, d), out_dtype),
        grid=(m // tm,),
        in_specs=[pl.BlockSpec((tm, d), lambda i: (i + off, 0)), pl.BlockSpec((1, d), lambda i: (0, 0))],
        out_specs=pl.BlockSpec((tm, d), lambda i: (i, 0)),
        compiler_params=_params("parallel"),
    )(x, g.reshape(1, d))


def _dot(a, b, precise, dims=None):
    dt, prec = (F32, lax.Precision.HIGHEST) if precise else (BF16, None)
    a, b = a.astype(dt), b.astype(dt)
    if dims is None:
        return jnp.dot(a, b, precision=prec, preferred_element_type=F32)
    return lax.dot_general(a, b, dims, precision=prec, preferred_element_type=F32)


def _mm_kernel(precise, x_ref, w_ref, o_ref):
    o_ref[...] = _dot(x_ref[...], w_ref[...], precise).astype(o_ref.dtype)


def _mm_res_kernel(precise, x_ref, w_ref, r_ref, o_ref):
    o_ref[...] = r_ref[...] + _dot(x_ref[...], w_ref[...], precise)


def matmul(x, w, n_out, out_dtype=F32, residual=None, row0=0, layer=None, precise=False):
    m, k = x.shape
    tm = _pick(int(np.gcd(m, row0)) if row0 else m, (2176, 1792, 1088, 1024, 896, 512, 256, 128))
    tn = _pick(n_out, (512, 256, 128))
    off = row0 // tm
    if layer is None:
        w_spec = pl.BlockSpec((k, tn), lambda i, j: (0, j))
    else:
        w_spec = pl.BlockSpec((None, k, tn), lambda i, j: (layer, 0, j))
    in_specs = [pl.BlockSpec((tm, k), lambda i, j: (i, 0)), w_spec]
    args = [x, w]
    body = _mm_kernel
    out_shape = jax.ShapeDtypeStruct((m, n_out), out_dtype)
    aliases = {}
    if residual is not None:
        in_specs.append(pl.BlockSpec((tm, tn), lambda i, j: (i + off, j)))
        args.append(residual)
        body = _mm_res_kernel
        out_shape = jax.ShapeDtypeStruct(residual.shape, residual.dtype)
        aliases = {2: 0}
    return pl.pallas_call(
        functools.partial(body, precise),
        out_shape=out_shape,
        grid=(m // tm, n_out // tn),
        in_specs=in_specs,
        out_specs=pl.BlockSpec((tm, tn), lambda i, j: (i + off, j)),
        input_output_aliases=aliases,
        compiler_params=_params("parallel", "parallel"),
    )(*args)


def _lg_kernel(precise, x_ref, wab_ref, wa2_ref, ba_ref, o_ref):
    ab = _dot(x_ref[...], wab_ref[...], precise)
    z = _dot(ab, wa2_ref[...], precise) + ba_ref[...]
    o_ref[...] = -_softplus(-z) * (1.0 / GLA_TAU)


def forget_gate(xn, w_ab, w_a2, b_a, precise=False):
    m, d = xn.shape
    tm = _pick(m, (1088, 896, 512, 256, 128))
    return pl.pallas_call(
        functools.partial(_lg_kernel, precise),
        out_shape=jax.ShapeDtypeStruct((m, GLA_KW), F32),
        grid=(m // tm,),
        in_specs=[pl.BlockSpec((tm, d), lambda i: (i, 0)),
                  pl.BlockSpec((d, BLK), lambda i: (0, 0)),
                  pl.BlockSpec((BLK, GLA_KW), lambda i: (0, 0)),
                  pl.BlockSpec((1, GLA_KW), lambda i: (0, 0))],
        out_specs=pl.BlockSpec((tm, GLA_KW), lambda i: (i, 0)),
        compiler_params=_params("parallel"),
    )(xn, w_ab, w_a2, b_a.reshape(1, GLA_KW))


SB_HEAD_GROUP = 4


def _later_and_total_matrix(width):
    row = lax.broadcasted_iota(jnp.int32, (width, width + BLK), 0)
    col = lax.broadcasted_iota(jnp.int32, (width, width + BLK), 1)
    return jnp.where(jnp.logical_or(row > col, col >= width), 1.0, 0.0).astype(BF16)


def _sb_prompt_kernel(bias_ref, q_ref, k_ref, v_ref, o_ref, acc_ref, run_ref):
    hg = pl.program_id(1)
    qi = pl.program_id(2)
    row = lax.broadcasted_iota(jnp.int32, (BLK, BLK), 0)
    col = lax.broadcasted_iota(jnp.int32, (BLK, BLK), 1)
    q_pos = qi * BLK + row
    scale = SB_HEAD_DIM ** -0.5
    acc_ref[...] = jnp.zeros_like(acc_ref)
    run_ref[...] = jnp.zeros_like(run_ref)

    def block(kb, n_blk, masked):
        width = n_blk * BLK
        off = pl.multiple_of(kb * BLK, BLK)
        sum_mat = _later_and_total_matrix(width)
        if masked:
            k_pos = kb * BLK + col
            mask = jnp.logical_and(k_pos < q_pos, k_pos >= FRONT_PAD)
        heads = range(SB_HEAD_GROUP)
        sls = [slice(h * SB_HEAD_DIM, (h + 1) * SB_HEAD_DIM) for h in heads]
        qs = [q_ref[:, sl].astype(BF16) for sl in sls]
        ks = [k_ref[pl.ds(off, width), sl].astype(BF16) for sl in sls]
        vs = [v_ref[pl.ds(off, width), sl].astype(BF16) for sl in sls]
        runs = [run_ref[h] for h in heads]
        accs = [acc_ref[h] for h in heads]
        nt = (((1,), (1,)), ((), ()))
        zs = [lax.dot_general(qs[h], ks[h], nt, preferred_element_type=F32) * scale
              + bias_ref[hg * SB_HEAD_GROUP + h] for h in heads]
        sps = [jnp.maximum(z, 0.0) + jnp.log(1.0 + jnp.exp(-jnp.abs(z))) for z in zs]
        stays = [jnp.where(mask, -sp, 0.0) if masked else -sp for sp in sps]
        splits = [_split_bf16(s) for s in stays]
        sums = [jnp.dot(hi, sum_mat, preferred_element_type=F32) + jnp.dot(lo, sum_mat, preferred_element_type=F32)
                for hi, lo in splits]
        run_w = [jnp.concatenate([r] * n_blk, axis=1) for r in runs]
        ws = [jnp.exp(zs[h] - sps[h] + run_w[h] + sums[h][:, :width]) for h in heads]
        if masked:
            ws = [jnp.where(mask, w, 0.0) for w in ws]
        pvs = [jnp.dot(ws[h].astype(BF16), vs[h], preferred_element_type=F32) for h in heads]
        for h in heads:
            acc_ref[h] = accs[h] + pvs[h]
            run_ref[h] = runs[h] + sums[h][:, width:]

    block(qi, 1, True)

    n_inner = jnp.maximum(qi - 1, 0)

    def body(i, carry):
        block(qi - 2 - 2 * i, 2, False)
        return carry

    lax.fori_loop(0, n_inner >> 1, body, 0)

    @pl.when((n_inner & 1) == 1)
    def _():
        block(1, 1, False)

    @pl.when(qi > 0)
    def _():
        block(0, 1, True)

    for h in range(SB_HEAD_GROUP):
        o_ref[:, h * SB_HEAD_DIM:(h + 1) * SB_HEAD_DIM] = acc_ref[h].astype(o_ref.dtype)


def sb_prompt(z_main, sb_bias, n_batch, rows_per_seq, m_total):
    nq = rows_per_seq // BLK
    gw = SB_HEAD_GROUP * SB_HEAD_DIM
    n_groups = SB_HEADS // SB_HEAD_GROUP
    kcol = COL_KA // gw
    vcol = COL_VA // gw
    return pl.pallas_call(
        _sb_prompt_kernel,
        out_shape=jax.ShapeDtypeStruct((m_total, SB_WIDTH), BF16),
        grid=(n_batch, n_groups, nq),
        in_specs=[pl.BlockSpec(memory_space=pltpu.SMEM),
                  pl.BlockSpec((BLK, gw), lambda b, g, i: (b * nq + i, g)),
                  pl.BlockSpec((rows_per_seq, gw), lambda b, g, i: (b, kcol + g)),
                  pl.BlockSpec((rows_per_seq, gw), lambda b, g, i: (b, vcol + g))],
        out_specs=pl.BlockSpec((BLK, gw), lambda b, g, i: (b * nq + i, g)),
        scratch_shapes=[pltpu.VMEM((SB_HEAD_GROUP, BLK, SB_HEAD_DIM), F32),
                        pltpu.VMEM((SB_HEAD_GROUP, BLK, BLK), F32)],
        compiler_params=_params("parallel", "parallel", "arbitrary"),
    )(sb_bias, z_main, z_main, z_main)


Q_ROWS = 16


PAGE_GROUP = 2


def _later_matrix(n_pages):
    idx = np.arange(n_pages * BLK)
    pj, ps = idx[:, None] // BLK, idx[None, :] // BLK
    later = (pj < ps) | ((pj == ps) & (idx[:, None] > idx[None, :]))
    return later.astype(np.float32)


def _sb_paged_kernel(pages_per_step, n_t, pt_ref, q_ref, bias_ref, lat1_ref, latg_ref, knew_ref, vnew_ref, *rest):
    k_refs = rest[:pages_per_step]
    v_refs = rest[pages_per_step:2 * pages_per_step]
    o_ref, acc_ref, run_ref = rest[2 * pages_per_step:]
    p = pl.program_id(1)
    n_rows = SB_HEADS * Q_ROWS
    scale = SB_HEAD_DIM ** -0.5
    nt = (((1,), (1,)), ((), ()))
    heads = range(SB_HEADS)

    def head_rows(refs, h):
        parts = []
        for r in refs:
            n_keys = r.shape[0] // SB_HEADS
            parts.append(r[pl.ds(h, n_keys, stride=SB_HEADS), :])
            if n_keys < BLK:
                parts.append(jnp.zeros((BLK - n_keys, SB_HEAD_DIM), F32))
        return jnp.concatenate(parts, axis=0).astype(BF16)

    def process(groups, later_ref, mask, run, accs):
        gs = range(len(groups))
        ks = [[head_rows(kl, h) for h in heads] for kl, _ in groups]
        vs = [[head_rows(vl, h) for h in heads] for _, vl in groups]
        bias = jnp.concatenate([bias_ref[...]] * len(groups[0][0]), axis=1)
        zs = [jnp.concatenate([lax.dot_general(q_ref[h], ks[g][h], nt, preferred_element_type=F32) for h in heads],
                              axis=0) * scale + bias for g in gs]
        sps = [jnp.maximum(z, 0.0) + jnp.log(1.0 + jnp.exp(-jnp.abs(z))) for z in zs]
        stays = [-sp if mask is None else jnp.where(mask, -sp, 0.0) for sp in sps]
        later = later_ref[...]
        sums = []
        for s in stays:
            hi, lo = _split_bf16(s)
            both = jnp.dot(jnp.concatenate([hi, lo], axis=0), later, preferred_element_type=F32)
            sums.append(both[:n_rows] + both[n_rows:])
        totals = [jnp.sum(s, axis=1, keepdims=True) for s in stays]
        ws = []
        for g in gs:
            w = jnp.exp(zs[g] - sps[g] + run + sums[g])
            ws.append((w if mask is None else jnp.where(mask, w, 0.0)).astype(BF16))
            run = run + totals[g]
        for g in gs:
            accs = [accs[h] + jnp.dot(ws[g][h * Q_ROWS:(h + 1) * Q_ROWS], vs[g][h], preferred_element_type=F32)
                    for h in heads]
        return run, accs

    @pl.when(p == 0)
    def _():
        t = lax.broadcasted_iota(jnp.int32, (n_rows, BLK), 0) & (Q_ROWS - 1)
        key = lax.broadcasted_iota(jnp.int32, (n_rows, BLK), 1)
        zero_acc = [jnp.zeros((Q_ROWS, SB_HEAD_DIM), F32) for _ in heads]
        run, accs = process([([knew_ref], [vnew_ref])], lat1_ref, key < t,
                            jnp.zeros((n_rows, 1), F32), zero_acc)
        run_ref[...] = run
        for h in heads:
            acc_ref[h] = accs[h]

    run = run_ref[...]
    accs = [acc_ref[h] for h in heads]
    groups = [(k_refs[g:g + PAGE_GROUP], v_refs[g:g + PAGE_GROUP]) for g in range(0, pages_per_step, PAGE_GROUP)]
    run, accs = process(groups, latg_ref, None, run, accs)
    run_ref[...] = run
    for h in heads:
        acc_ref[h] = accs[h]

    @pl.when(p == pl.num_programs(1) - 1)
    def _():
        valid = lax.broadcasted_iota(jnp.int32, (SAMPLE_ROWS, SB_HEAD_DIM), 0) < n_t
        for h in heads:
            o_ref[:, h * SB_HEAD_DIM:(h + 1) * SB_HEAD_DIM] = jnp.where(valid, accs[h][:SAMPLE_ROWS], 0.0)


def sb_paged(q_s, k_new, v_new, cache_k, cache_v, page_table, sb_bias, layer):
    n_seq, n_t = q_s.shape[0], q_s.shape[1]
    n_pages = page_table.shape[1]
    page = cache_k.shape[2]
    assert page == BLK and n_t <= SAMPLE_ROWS <= Q_ROWS and n_pages % PAGE_GROUP == 0
    pps = max(c for c in (4 * PAGE_GROUP, 2 * PAGE_GROUP, PAGE_GROUP) if n_pages % c == 0)
    later_1 = jnp.asarray(_later_matrix(1), dtype=BF16)
    later_g = jnp.asarray(_later_matrix(PAGE_GROUP), dtype=BF16)
    n_rows = SB_HEADS * Q_ROWS
    page_rows = page * SB_HEADS
    q = jnp.pad(jnp.transpose(q_s, (0, 2, 1, 3)), ((0, 0), (0, 0), (0, Q_ROWS - n_t), (0, 0))).astype(BF16)
    bias = jnp.broadcast_to(jnp.repeat(sb_bias.astype(F32), Q_ROWS)[:, None], (n_rows, BLK))
    pad = ((0, 0), (0, SAMPLE_ROWS - n_t), (0, 0))
    new_rows = SAMPLE_ROWS * SB_HEADS
    k_new = jnp.pad(k_new, pad).reshape(n_seq, new_rows, SB_HEAD_DIM)
    v_new = jnp.pad(v_new, pad).reshape(n_seq, new_rows, SB_HEAD_DIM)
    ck = cache_k.reshape(cache_k.shape[0], cache_k.shape[1], page_rows, SB_HEAD_DIM)
    cv = cache_v.reshape(cache_v.shape[0], cache_v.shape[1], page_rows, SB_HEAD_DIM)

    def page_spec(j):
        return pl.BlockSpec((None, None, page_rows, SB_HEAD_DIM),
                            lambda s, p, pt: (layer, pt[s, n_pages - 1 - (p * pps + j)], 0, 0))

    in_specs = ([pl.BlockSpec((None, SB_HEADS, Q_ROWS, SB_HEAD_DIM), lambda s, p, pt: (s, 0, 0, 0)),
                 pl.BlockSpec((n_rows, BLK), lambda s, p, pt: (0, 0)),
                 pl.BlockSpec(later_1.shape, lambda s, p, pt: (0, 0)),
                 pl.BlockSpec(later_g.shape, lambda s, p, pt: (0, 0)),
                 pl.BlockSpec((None, new_rows, SB_HEAD_DIM), lambda s, p, pt: (s, 0, 0)),
                 pl.BlockSpec((None, new_rows, SB_HEAD_DIM), lambda s, p, pt: (s, 0, 0))]
                + [page_spec(j) for j in range(pps)] + [page_spec(j) for j in range(pps)])
    out = pl.pallas_call(
        functools.partial(_sb_paged_kernel, pps, n_t),
        out_shape=jax.ShapeDtypeStruct((n_seq, SAMPLE_ROWS, SB_WIDTH), F32),
        grid_spec=pltpu.PrefetchScalarGridSpec(
            num_scalar_prefetch=1,
            grid=(n_seq, n_pages // pps),
            in_specs=in_specs,
            out_specs=pl.BlockSpec((None, SAMPLE_ROWS, SB_WIDTH), lambda s, p, pt: (s, 0, 0)),
            scratch_shapes=[pltpu.VMEM((SB_HEADS, Q_ROWS, SB_HEAD_DIM), F32), pltpu.VMEM((n_rows, 1), F32)],
        ),
        compiler_params=_params("parallel", "arbitrary"),
    )(page_table, q, bias, later_1, later_g, k_new, v_new, *([ck] * pps), *([cv] * pps))
    return out.reshape(n_seq * SAMPLE_ROWS, SB_WIDTH).astype(BF16)


GLA_LEVELS = (64, 32, 16, 8, 4, 2, 1)


def _gla_static_matrix(levels):
    idx = np.arange(BLK)
    mats = [(idx[None, :] <= idx[:, None]), (idx[None, :] > idx[:, None]), np.ones((BLK, BLK), bool)]
    for size in levels:
        blk = idx // size
        odd = (blk % 2) == 1
        ref = np.where(odd, blk * size - 1, (blk + 1) * size - 1)
        j = idx[None, :]
        q_side = (j > ref[:, None]) & (j <= idx[:, None])
        k_side = (j > idx[:, None]) & (j <= ref[:, None])
        mats.append(np.where(odd[:, None], q_side, k_side))
    return np.concatenate(mats, axis=0).astype(np.float32)


def _gla_kernel(levels, precise, n_valid, sm_ref, q_ref, k_ref, v_ref, g_ref, s0_ref, o_ref, sout_ref, state_ref):
    c = pl.program_id(2)
    rows_in = q_ref.shape[0]

    @pl.when(c == 0)
    def _():
        state_ref[...] = s0_ref[...]

    def load(ref):
        x = ref[...]
        if rows_in < BLK:
            x = jnp.concatenate([x, jnp.zeros((BLK - rows_in, x.shape[1]), F32)], axis=0)
        return x

    g = load(g_ref)
    if n_valid < BLK:
        g = jnp.where(lax.broadcasted_iota(jnp.int32, g.shape, 0) < n_valid, g, 0.0)
    g_hi, g_lo = _split_bf16(g)
    sm = sm_ref[...]
    expo = (jnp.dot(sm, g_hi, preferred_element_type=F32) + jnp.dot(sm, g_lo, preferred_element_type=F32))
    q = load(q_ref) * (GLA_DK ** -0.5)
    k = load(k_ref)
    v = load(v_ref)
    s_prev = state_ref[...]
    b = expo[0:BLK]
    rem = expo[BLK:2 * BLK]
    total = expo[2 * BLK:3 * BLK]

    row = lax.broadcasted_iota(jnp.int32, (BLK, BLK), 0)
    col = lax.broadcasted_iota(jnp.int32, (BLK, BLK), 1)
    nt = (((1,), (1,)), ((), ()))
    scores = jnp.where(row == col, jnp.sum(q * k, axis=1, keepdims=True), 0.0)
    for li, size in enumerate(levels):
        f = jnp.exp(expo[(3 + li) * BLK:(4 + li) * BLK])
        part = _dot(q * f, k * f, precise, nt)
        shift = size.bit_length() - 1
        rb = row >> shift
        pair = jnp.logical_and((rb & 1) == 1, (col >> shift) == rb - 1)
        scores = jnp.where(pair, part, scores)

    o_ref[...] = (_dot(q * jnp.exp(b), s_prev, precise) + _dot(scores, v, precise))[:rows_in]
    kv = _dot(jnp.transpose(k * jnp.exp(rem)), v, precise)
    decay = jnp.exp(jnp.transpose(total)[:, 0:1])
    s_new = decay * s_prev + kv
    state_ref[...] = s_new
    sout_ref[...] = s_new


def gla(q_arr, k_arr, v_arr, g_arr, q_col, k_col, v_col, s0, n_batch, n_chunks, m_out,
        levels=GLA_LEVELS, precise=False, rows=BLK, n_valid=BLK):
    sm = jnp.asarray(_gla_static_matrix(levels), dtype=BF16)
    qc, kc, vc = q_col // GLA_DK, k_col // GLA_DK, v_col // GLA_DV
    return pl.pallas_call(
        functools.partial(_gla_kernel, levels, precise, n_valid),
        out_shape=(jax.ShapeDtypeStruct((m_out, GLA_VW), F32),
                   jax.ShapeDtypeStruct((n_batch, GLA_HEADS, GLA_DK, GLA_DV), F32)),
        grid=(n_batch, GLA_HEADS, n_chunks),
        in_specs=[pl.BlockSpec(sm.shape, lambda b, h, c: (0, 0)),
                  pl.BlockSpec((rows, GLA_DK), lambda b, h, c: (b * n_chunks + c, qc + h)),
                  pl.BlockSpec((rows, GLA_DK), lambda b, h, c: (b * n_chunks + c, kc + h)),
                  pl.BlockSpec((rows, GLA_DV), lambda b, h, c: (b * n_chunks + c, vc + h)),
                  pl.BlockSpec((rows, GLA_DK), lambda b, h, c: (b * n_chunks + c, h)),
                  pl.BlockSpec((None, None, GLA_DK, GLA_DV), lambda b, h, c: (b, h, 0, 0))],
        out_specs=(pl.BlockSpec((rows, GLA_DV), lambda b, h, c: (b * n_chunks + c, h)),
                   pl.BlockSpec((None, None, GLA_DK, GLA_DV), lambda b, h, c: (b, h, 0, 0))),
        scratch_shapes=[pltpu.VMEM((GLA_DK, GLA_DV), F32)],
        compiler_params=_params("parallel", "parallel", "arbitrary"),
    )(sm, q_arr, k_arr, v_arr, g_arr, s0)


def _merge_kernel(precise, oa_ref, ob_ref, rb_ref, gn_ref, ga_ref, gb_ref, wa_ref, wb_ref, o_ref, obn_ref):
    @pl.when(pl.program_id(1) == 0)
    def _():
        for h in range(GLA_HEADS):
            sl = slice(h * GLA_DV, (h + 1) * GLA_DV)
            x = ob_ref[:, sl]
            y = x * lax.rsqrt(jnp.mean(x * x, axis=-1, keepdims=True) + EPS) * gn_ref[...]
            r = rb_ref[:, sl]
            obn_ref[:, sl] = (y * (r * _sigmoid(r))).astype(obn_ref.dtype)

    a = _dot(oa_ref[...], wa_ref[...], precise)
    b = _dot(obn_ref[...], wb_ref[...], precise)
    o_ref[...] = (_sigmoid(ga_ref[...]) * a + _sigmoid(gb_ref[...]) * b).astype(o_ref.dtype)


def merge_mix(o_a, o_b, z_main, z_gates, gla_norm_g, w_a_up, w_b_up, layer, d_model, precise=False):
    m = o_a.shape[0]
    tm = _pick(m, (1088, 896, 512, 256, 128))
    tn = _pick(d_model, (512, 256, 128))
    nb = d_model // tn
    rb_blk = COL_RB // GLA_VW
    act = F32 if precise else BF16
    return pl.pallas_call(
        functools.partial(_merge_kernel, precise),
        out_shape=jax.ShapeDtypeStruct((m, d_model), act),
        grid=(m // tm, nb),
        in_specs=[pl.BlockSpec((tm, SB_WIDTH), lambda i, j: (i, 0)),
                  pl.BlockSpec((tm, GLA_VW), lambda i, j: (i, 0)),
                  pl.BlockSpec((tm, GLA_VW), lambda i, j: (i, rb_blk)),
                  pl.BlockSpec((1, GLA_DV), lambda i, j: (0, 0)),
                  pl.BlockSpec((tm, tn), lambda i, j: (i, j)),
                  pl.BlockSpec((tm, tn), lambda i, j: (i, nb + j)),
                  pl.BlockSpec((None, SB_WIDTH, tn), lambda i, j: (layer, 0, j)),
                  pl.BlockSpec((None, GLA_VW, tn), lambda i, j: (layer, 0, j))],
        out_specs=pl.BlockSpec((tm, tn), lambda i, j: (i, j)),
        scratch_shapes=[pltpu.VMEM((tm, GLA_VW), act)],
        compiler_params=_params("parallel", "arbitrary"),
    )(o_a, o_b, z_main, gla_norm_g.reshape(1, GLA_DV), z_gates, z_gates, w_a_up, w_b_up)


def _router_kernel(h_ref, g_ref, wr_ref, br_ref, xn_ref, route_ref):
    x = h_ref[...]
    t = x * lax.rsqrt(jnp.mean(x * x, axis=-1, keepdims=True) + EPS) * g_ref[...]
    tm, n_sub = x.shape[0], x.shape[1] // BLK
    for s in range(n_sub):
        xn_ref[pl.ds(s, tm, stride=n_sub), :] = t[:, s * BLK:(s + 1) * BLK]
    logits = jnp.dot(t, wr_ref[...], precision=lax.Precision.HIGHEST, preferred_element_type=F32) + br_ref[...]
    lane = lax.broadcasted_iota(jnp.int32, logits.shape, 1)
    is_group = lane < N_GROUPS
    gl = jnp.where(is_group, logits, NEG_BIG)
    g_max = jnp.max(gl, axis=1, keepdims=True)
    g_idx = jnp.min(jnp.where(gl == g_max, lane, BLK), axis=1, keepdims=True)
    g_sum = jnp.sum(jnp.where(is_group, jnp.exp(gl - g_max), 0.0), axis=1, keepdims=True)
    g_w = 1.0 / g_sum
    lo = N_GROUPS + EXPERTS_PER_GROUP * g_idx
    in_grp = jnp.logical_and(lane >= lo, lane < lo + EXPERTS_PER_GROUP)
    e1 = jnp.where(in_grp, logits, NEG_BIG)
    v1 = jnp.max(e1, axis=1, keepdims=True)
    i1 = jnp.min(jnp.where(e1 == v1, lane, BLK), axis=1, keepdims=True)
    e2 = jnp.where(lane == i1, NEG_BIG, e1)
    v2 = jnp.max(e2, axis=1, keepdims=True)
    i2 = jnp.min(jnp.where(jnp.logical_and(e2 == v2, in_grp), lane, BLK), axis=1, keepdims=True)
    r = jnp.exp(v2 - v1)
    w1 = g_w / (1.0 + r)
    w2 = w1 * r
    route = jnp.where(lane == 0, (i1 - N_GROUPS).astype(F32),
                      jnp.where(lane == 1, (i2 - N_GROUPS).astype(F32),
                                jnp.where(lane == 2, w1, jnp.where(lane == 3, w2, 0.0))))
    route_ref[...] = route


def router(h, norm_g, w_r, b_r):
    m, d = h.shape
    tm = _pick(m, (256, 128, 64, 16))
    n_sub = d // BLK
    return pl.pallas_call(
        _router_kernel,
        out_shape=(jax.ShapeDtypeStruct((m * n_sub, BLK), F32), jax.ShapeDtypeStruct((m, BLK), F32)),
        grid=(m // tm,),
        in_specs=[pl.BlockSpec((tm, d), lambda i: (i, 0)),
                  pl.BlockSpec((1, d), lambda i: (0, 0)),
                  pl.BlockSpec((d, BLK), lambda i: (0, 0)),
                  pl.BlockSpec((1, BLK), lambda i: (0, 0))],
        out_specs=(pl.BlockSpec((tm * n_sub, BLK), lambda i: (i, 0)), pl.BlockSpec((tm, BLK), lambda i: (i, 0))),
        compiler_params=_params("parallel"),
    )(h, norm_g.reshape(1, d), w_r, b_r)


def _expert_kernel(te_ref, nu_ref, slot_ref, x_hbm, sw_ref, wg_ref, wu_ref, wd_ref, o_ref, xbuf, sem):
    i = pl.program_id(0)
    n_used = nu_ref[0]
    cur = i % 2
    n_sub = x_hbm.shape[1]

    def gather_row(tile, buf, r):
        token = slot_ref[tile * EXPERT_TILE + r] >> 1
        dst = xbuf.at[buf, pl.ds(pl.multiple_of(r * n_sub, n_sub), n_sub), :]
        return pltpu.make_async_copy(x_hbm.at[token], dst, sem.at[buf])

    def start_gather(tile, buf):
        def body(r, carry):
            gather_row(tile, buf, r).start(priority=1)
            return carry
        lax.fori_loop(0, EXPERT_TILE, body, 0)

    @pl.when(i == 0)
    def _():
        start_gather(0, 0)

    @pl.when(i + 1 < n_used)
    def _():
        start_gather(i + 1, 1 - cur)

    @pl.when(i < n_used)
    def _():
        pltpu.make_async_copy(xbuf.at[cur], xbuf.at[cur], sem.at[cur]).wait()
        x = jnp.concatenate([xbuf[cur, pl.ds(s, EXPERT_TILE, stride=n_sub), :] for s in range(n_sub)],
                            axis=1).astype(BF16)
        gate =jnp.dot(x, wg_ref[...].astype(BF16), preferred_element_type=F32)
        up = jnp.dot(x, wu_ref[...].astype(BF16), preferred_element_type=F32)
        hid = (gate * _sigmoid(gate)) * up
        hid = (hid * sw_ref[...]).astype(BF16)
        o_ref[...] = jnp.dot(hid, wd_ref[...].astype(BF16), preferred_element_type=F32)

    @pl.when(i >= n_used)
    def _():
        o_ref[...] = jnp.zeros_like(o_ref)


def expert_mlp(x, row_slot, slot_w, tile_expert, n_used, w_g, w_u, w_d, layer):
    n_sub = x.shape[1]
    d = n_sub * BLK
    r = row_slot.shape[0]
    nt = r // EXPERT_TILE

    def w_map(i, te, nu, sl):
        return (layer, te[i], 0, 0)

    return pl.pallas_call(
        _expert_kernel,
        out_shape=jax.ShapeDtypeStruct((r, d), F32),
        grid_spec=pltpu.PrefetchScalarGridSpec(
            num_scalar_prefetch=3,
            grid=(nt,),
            in_specs=[pl.BlockSpec(memory_space=pl.ANY),
                      pl.BlockSpec((EXPERT_TILE, 1), lambda i, te, nu, sl: (jnp.minimum(i, nu[0] - 1), 0)),
                      pl.BlockSpec((None, None, d, D_EXPERT), w_map),
                      pl.BlockSpec((None, None, d, D_EXPERT), w_map),
                      pl.BlockSpec((None, None, D_EXPERT, d), w_map)],
            out_specs=pl.BlockSpec((EXPERT_TILE, d), lambda i, te, nu, sl: (i, 0)),
            scratch_shapes=[pltpu.VMEM((2, EXPERT_TILE * n_sub, BLK), F32), pltpu.SemaphoreType.DMA((2,))],
        ),
        compiler_params=pltpu.CompilerParams(dimension_semantics=("arbitrary",), vmem_limit_bytes=VMEM_LIMIT,
                                             disable_bounds_checks=True),
    )(tile_expert, n_used, row_slot, x, slot_w, w_g, w_u, w_d)


def moe_dispatch(route, m):
    e_flat = route[:, 0:2].astype(jnp.int32).reshape(-1)
    w_flat = route[:, 2:4].reshape(-1)
    n_slots = 2 * m
    n_rows = n_slots + N_EXPERTS * EXPERT_TILE
    onehot = (e_flat[:, None] == jnp.arange(N_EXPERTS, dtype=jnp.int32)[None, :]).astype(jnp.int32)
    running = jnp.cumsum(onehot, axis=0)
    counts = running[-1]
    rank = jnp.sum(onehot * running, axis=1) - 1
    padded = ((counts + EXPERT_TILE - 1) // EXPERT_TILE) * EXPERT_TILE
    ends_p = jnp.cumsum(padded)
    starts_p = ends_p - padded
    pos = jnp.sum(onehot * starts_p[None, :], axis=1) + rank
    row_slot = jnp.zeros((n_rows,), jnp.int32).at[pos].set(jnp.arange(n_slots, dtype=jnp.int32))
    slot_w = jnp.take(w_flat, row_slot).reshape(n_rows, 1)
    n_used = (ends_p[-1] // EXPERT_TILE).astype(jnp.int32)
    tile_start = jnp.arange(n_rows // EXPERT_TILE, dtype=jnp.int32) * EXPERT_TILE
    tile_expert = jnp.sum((jnp.minimum(tile_start, ends_p[-1] - 1)[:, None] >= ends_p[None, :]).astype(jnp.int32),
                          axis=1)
    tile_expert = jnp.minimum(tile_expert, N_EXPERTS - 1).astype(jnp.int32)
    return row_slot, slot_w, pos.reshape(m, 2), tile_expert, n_used.reshape(1)


def _assemble_kernel(n_chunks, n_prompt_blocks, xp_ref, meta_ref, xs_ref, o_ref):
    r = pl.program_id(0)
    is_prompt = r < n_prompt_blocks
    first = lax.rem(r, n_chunks) == 0

    @pl.when(jnp.logical_not(is_prompt))
    def _():
        o_ref[...] = xs_ref[...]

    @pl.when(jnp.logical_and(is_prompt, first))
    def _():
        o_ref[:FRONT_PAD, :] = jnp.zeros((FRONT_PAD, o_ref.shape[1]), o_ref.dtype)
        o_ref[FRONT_PAD:, :] = meta_ref[...]

    @pl.when(jnp.logical_and(is_prompt, jnp.logical_not(first)))
    def _():
        o_ref[...] = xp_ref[...]


def assemble_tokens(x_prompt, meta, h_s):
    n_b, seq, d = x_prompt.shape
    m_sample = h_s.shape[0]
    n_chunks = seq // BLK + 1
    n_prompt_blocks = n_b * n_chunks
    if m_sample % BLK:
        meta_rows = jnp.broadcast_to(meta[None], (n_b, N_META, d))
        h_p = jnp.concatenate([jnp.zeros((n_b, FRONT_PAD, d), F32), meta_rows, x_prompt], axis=1)
        return jnp.concatenate([h_p.reshape(n_prompt_blocks * BLK, d), h_s], axis=0)
    return pl.pallas_call(
        functools.partial(_assemble_kernel, n_chunks, n_prompt_blocks),
        out_shape=jax.ShapeDtypeStruct((n_prompt_blocks * BLK + m_sample, d), F32),
        grid=(n_prompt_blocks + m_sample // BLK,),
        in_specs=[pl.BlockSpec((None, BLK, d), lambda r: (jnp.minimum(r // n_chunks, n_b - 1),
                                                          jnp.maximum(lax.rem(r, n_chunks) - 1, 0), 0)),
                  pl.BlockSpec((N_META, d), lambda r: (0, 0)),
                  pl.BlockSpec((BLK, d), lambda r: (jnp.maximum(r - n_prompt_blocks, 0), 0))],
        out_specs=pl.BlockSpec((BLK, d), lambda r: (r, 0)),
        compiler_params=_params("parallel"),
    )(x_prompt, meta, h_s)


def kernel(x_prompt, x_sample, cache_k, cache_v, state_gla, page_table, meta, norm1_g, w_in, sb_bias, gla_w_a2, gla_b_a, gla_norm_g, w_a_up, w_b_up, w_out, norm2_g, w_router_group, b_router_group, w_router_expert, b_router_expert, w_exp_gate, w_exp_up, w_exp_down, final_norm_g):
    n_b, seq, d = x_prompt.shape
    n_s, n_t, _ = x_sample.shape
    depth = w_in.shape[0]
    assert seq % BLK == 0 and n_t <= SAMPLE_ROWS
    rows_per_seq = seq + BLK
    n_chunks = rows_per_seq // BLK
    m_prompt = n_b * rows_per_seq
    m = m_prompt + n_s * SAMPLE_ROWS

    h_s = jnp.pad(x_sample, ((0, 0), (0, SAMPLE_ROWS - n_t), (0, 0))).reshape(n_s * SAMPLE_ROWS, d)
    h = assemble_tokens(x_prompt, meta.astype(F32), h_s)

    zero_state = jnp.zeros((n_b, GLA_HEADS, GLA_DK, GLA_DV), F32)
    kp_rows, vp_rows, ks_rows, vs_rows, sp_list, ss_list = [], [], [], [], [], []
    m_sample = n_s * SAMPLE_ROWS
    sample_levels = tuple(s for s in GLA_LEVELS if s < pl.next_power_of_2(n_t))
    for l in range(depth):
        xn_p = rms_norm_rows(h, norm1_g[l], BF16, 0, m_prompt)
        xn_s = rms_norm_rows(h, norm1_g[l], F32, m_prompt, m_sample)
        w_gates = w_in[l][:, COL_GATES:]
        w_ab = jnp.pad(w_in[l][:, COL_AB:COL_AB + GLA_RANK], ((0, 0), (0, BLK - GLA_RANK)))
        w_a2 = jnp.pad(gla_w_a2[l], ((0, BLK - GLA_RANK), (0, 0)))
        z_main = matmul(xn_p, w_in, N_MAIN, layer=l)
        z_gates = matmul(xn_p, w_gates, 2 * d)
        lg = forget_gate(xn_p, w_ab, w_a2, gla_b_a[l])
        zs_main = matmul(xn_s, w_in, N_MAIN, layer=l, precise=True)
        zs_gates = matmul(xn_s, w_gates, 2 * d, precise=True)
        lgs = forget_gate(xn_s, w_ab, w_a2, gla_b_a[l], precise=True)

        z_s = zs_main.reshape(n_s, SAMPLE_ROWS, N_MAIN)[:, :n_t]
        ks_new = z_s[:, :, COL_KA:COL_KA + SB_WIDTH]
        vs_new = z_s[:, :, COL_VA:COL_VA + SB_WIDTH]

        o_a_p = sb_prompt(z_main, sb_bias[l], n_b, rows_per_seq, m_prompt)
        q_s = z_s[:, :, COL_QA:COL_QA + SB_WIDTH].reshape(n_s, n_t, SB_HEADS, SB_HEAD_DIM)
        o_a_s = sb_paged(q_s, ks_new, vs_new, cache_k, cache_v, page_table, sb_bias[l], l)

        o_b_p, s_p = gla(z_main, z_main, z_main, lg, COL_QB, COL_KB, COL_VB, zero_state, n_b, n_chunks, m_prompt)
        o_b_s, s_s = gla(zs_main, zs_main, zs_main, lgs, COL_QB, COL_KB, COL_VB, state_gla[l], n_s, 1, m_sample,
                         levels=sample_levels, precise=True, rows=SAMPLE_ROWS, n_valid=n_t)

        mix_p = merge_mix(o_a_p, o_b_p, z_main, z_gates, gla_norm_g[l], w_a_up, w_b_up, l, d)
        mix_s = merge_mix(o_a_s, o_b_s, zs_main, zs_gates, gla_norm_g[l], w_a_up, w_b_up, l, d, precise=True)
        h = matmul(mix_p, w_out, d, residual=h, layer=l)
        h = matmul(mix_s, w_out, d, residual=h, row0=m_prompt, layer=l, precise=True)

        w_r = jnp.pad(jnp.concatenate([w_router_group[l], w_router_expert[l]], axis=1),
                      ((0, 0), (0, BLK - N_GROUPS - N_EXPERTS)))
        b_r = jnp.pad(jnp.concatenate([b_router_group[l], b_router_expert[l]]),
                      (0, BLK - N_GROUPS - N_EXPERTS)).reshape(1, BLK)
        xn2, route = router(h, norm2_g[l], w_r, b_r)
        row_slot, slot_w, pos, tile_expert, n_used = moe_dispatch(route, m)
        y_slots = expert_mlp(xn2.reshape(m, d // BLK, BLK), row_slot, slot_w, tile_expert, n_used,
                             w_exp_gate, w_exp_up, w_exp_down, l)
        h = h + jnp.take(y_slots, pos[:, 0], axis=0) + jnp.take(y_slots, pos[:, 1], axis=0)

        kv_p = z_main[:m_prompt].reshape(n_b, rows_per_seq, N_MAIN)[:, FRONT_PAD:]
        kp_rows.append(kv_p[:, :, COL_KA:COL_KA + SB_WIDTH].reshape(n_b, -1, SB_HEADS, SB_HEAD_DIM))
        vp_rows.append(kv_p[:, :, COL_VA:COL_VA + SB_WIDTH].reshape(n_b, -1, SB_HEADS, SB_HEAD_DIM))
        ks_rows.append(ks_new.reshape(n_s, n_t, SB_HEADS, SB_HEAD_DIM))
        vs_rows.append(vs_new.reshape(n_s, n_t, SB_HEADS, SB_HEAD_DIM))
        sp_list.append(s_p)
        ss_list.append(s_s)

    yn = rms_norm_rows(h, final_norm_g, F32)
    y_prompt = yn[:m_prompt].reshape(n_b, rows_per_seq, d)[:, BLK:]
    y_sample = yn[m_prompt:].reshape(n_s, SAMPLE_ROWS, d)[:, :n_t]
    return (y_prompt, y_sample, jnp.stack(kp_rows), jnp.stack(vp_rows), jnp.stack(ks_rows),
            jnp.stack(vs_rows), jnp.stack(sp_list), jnp.stack(ss_list))
```
